```python
import math
import jax, jax.numpy as jnp
from jax import lax
import numpy as np

D_MODEL = 1024
BATCH = 16
SEQ = 4096
DEPTH = 4

N_HEADS_A = 16
N_KV_A = 4
HEAD_DIM_A = 64
WINDOW = 128
N_HEADS_B = 8
KEY_DIM_B = 128
VAL_DIM_B = 128
CONV_K = 4
CHUNK = 64
D_FF = 2816
NUM_BUCKETS = 32
MAX_DISTANCE = 128
DN_ALPHA = (2 * DEPTH) ** 0.25
DN_BETA = (8 * DEPTH) ** -0.25
LN_EPS = 1e-5
NORM_EPS = 1e-6
NEG_INF = -1e30

Q_A = N_HEADS_A * HEAD_DIM_A
KV_W = N_KV_A * HEAD_DIM_A
QK_B = N_HEADS_B * KEY_DIM_B
V_B = N_HEADS_B * VAL_DIM_B
CONV_CH = 2 * QK_B + V_B
SPLIT_SIZES = (Q_A, KV_W, KV_W, CONV_CH, N_HEADS_B, N_HEADS_B, V_B, 2 * D_MODEL)
N_IN = sum(SPLIT_SIZES)
SPLIT_POINTS = tuple(int(s) for s in np.cumsum(SPLIT_SIZES)[:-1])

kernel_name = "hybrid_swa_sink_gated_deltanet_macaron_deepnorm"


def layernorm(x, g, b):
    xf = x.astype(jnp.float32)
    mu = xf.mean(-1, keepdims=True)
    var = jnp.square(xf - mu).mean(-1, keepdims=True)
    y = (xf - mu) * lax.rsqrt(var + LN_EPS) * g.astype(jnp.float32) + b.astype(jnp.float32)
    return y.astype(x.dtype)


def swiglu(x, w13, w2):
    gate, up = jnp.split(x @ w13, 2, axis=-1)
    return (jax.nn.silu(gate) * up) @ w2


def t5_bucket(rel):
    n = jnp.maximum(rel, 0)
    max_exact = NUM_BUCKETS // 2
    nf = jnp.maximum(n, 1).astype(jnp.float32)
    large = max_exact + (jnp.log(nf / max_exact) / math.log(MAX_DISTANCE / max_exact)
                         * (NUM_BUCKETS - max_exact)).astype(jnp.int32)
    large = jnp.minimum(large, NUM_BUCKETS - 1)
    return jnp.where(n < max_exact, n, large)


def band_rel():
    r = jnp.arange(WINDOW)[:, None]
    j = jnp.arange(2 * WINDOW)[None, :]
    return r + WINDOW - j


def sliding_window_attention(q, k, v, pos_bias, sink):
    b, l = q.shape[:2]
    nb = l // WINDOW
    grp = N_HEADS_A // N_KV_A
    qb = (q.astype(jnp.float32) * HEAD_DIM_A ** -0.5).reshape(b, nb, WINDOW, N_KV_A, grp, HEAD_DIM_A)
    kb = k.astype(jnp.float32).reshape(b, nb, WINDOW, N_KV_A, HEAD_DIM_A)
    vb = v.astype(jnp.float32).reshape(b, nb, WINDOW, N_KV_A, HEAD_DIM_A)

    def with_prev(t):
        prev = jnp.concatenate([jnp.zeros_like(t[:, :1]), t[:, :-1]], axis=1)
        return jnp.concatenate([prev, t], axis=2)

    kk, vv = with_prev(kb), with_prev(vb)
    rel = band_rel()
    in_band = (rel >= 0) & (rel < WINDOW)
    key_in_own_block = jnp.arange(2 * WINDOW) >= WINDOW
    bias = pos_bias.reshape(N_KV_A, grp, WINDOW, 2 * WINDOW)
    sk = sink.astype(jnp.float32).reshape(N_KV_A, grp, 1, 1)

    def one_block(args):
        qi, ki, vi, i = args
        s = jnp.einsum('bqkgd,bskd->bkgqs', qi, ki) + bias
        valid = in_band & ((i > 0) | key_in_own_block)
        s = jnp.where(valid, s, NEG_INF)
        m = jnp.maximum(s.max(-1, keepdims=True), sk)
        p = jnp.exp(s - m)
        p = p / (p.sum(-1, keepdims=True) + jnp.exp(sk - m))
        return jnp.einsum('bkgqs,bskd->bqkgd', p, vi)

    o = lax.map(one_block, (jnp.moveaxis(qb, 1, 0), jnp.moveaxis(kk, 1, 0),
                            jnp.moveaxis(vv, 1, 0), jnp.arange(nb)))
    return jnp.moveaxis(o, 0, 1).reshape(b, l, Q_A)


def causal_depthwise_conv(u, w):
    return lax.conv_general_dilated(u, w[:, None, :].astype(u.dtype), window_strides=(1,),
                                    padding=[(CONV_K - 1, 0)],
                                    dimension_numbers=('NWC', 'WIO', 'NWC'),
                                    feature_group_count=u.shape[-1])


def l2norm(t):
    t = t.astype(jnp.float32)
    return t * lax.rsqrt(jnp.sum(t * t, -1, keepdims=True) + NORM_EPS)


def gated_delta_rule(q, k, v, g, beta):
    b, l, h, dk = q.shape
    dv = v.shape[-1]
    n = l // CHUNK

    def chunked(t):
        return t.reshape(b, n, CHUNK, h, t.shape[-1]).transpose(0, 3, 1, 2, 4)

    q = chunked(q * dk ** -0.5)
    k = chunked(k)
    v = chunked(v)
    g = jnp.cumsum(g.reshape(b, n, CHUNK, h).transpose(0, 3, 1, 2), axis=-1)
    beta = beta.reshape(b, n, CHUNK, h).transpose(0, 3, 1, 2)[..., None]
    kb, vb = k * beta, v * beta

    causal = jnp.tril(jnp.ones((CHUNK, CHUNK), bool))
    strict = jnp.tril(jnp.ones((CHUNK, CHUNK), bool), -1)
    decay = jnp.exp(jnp.where(causal, g[..., :, None] - g[..., None, :], NEG_INF))
    lower = jnp.where(strict, jnp.einsum('bhncd,bhnmd->bhncm', kb, k) * decay, 0.0)
    eye = jnp.eye(CHUNK, dtype=jnp.float32)
    t_inv = lax.linalg.triangular_solve(lower + eye, jnp.broadcast_to(eye, lower.shape),
                                        left_side=True, lower=True, unit_diagonal=True)
    u = t_inv @ vb
    w = t_inv @ (kb * jnp.exp(g)[..., None])
    a_intra = jnp.where(causal, jnp.einsum('bhncd,bhnmd->bhncm', q, k) * decay, 0.0)
    g_last = g[..., -1]
    k_tail = k * jnp.exp(g_last[..., None] - g)[..., None]
    q_dec = q * jnp.exp(g)[..., None]

    def step(s, xs):
        qd, kt, uc, wc, ac, gl = xs
        v_new = uc - jnp.einsum('bhcd,bhde->bhce', wc, s)
        o = jnp.einsum('bhcd,bhde->bhce', qd, s) + jnp.einsum('bhcm,bhme->bhce', ac, v_new)
        s = s * jnp.exp(gl)[..., None, None] + jnp.einsum('bhcd,bhce->bhde', kt, v_new)
        return s, o

    xs = tuple(jnp.moveaxis(t, 2, 0) for t in (q_dec, k_tail, u, w, a_intra, g_last))
    s0 = jnp.zeros((b, h, dk, dv), jnp.float32)
    _, o = lax.scan(step, s0, xs)
    return o.transpose(1, 0, 3, 2, 4).reshape(b, l, h, dv)


def hybrid_mixer(x, w_in, conv_w, a_log, dt_bias, dn_norm_g, sink,
                 w_branch_a, w_branch_b, w_out, pos_bias):
    b, l, _ = x.shape
    hcat = x @ w_in
    qa, ka, va, qkv_b, beta_raw, dt_raw, z, gate_raw = jnp.split(hcat, SPLIT_POINTS, axis=-1)

    ya = sliding_window_attention(qa.reshape(b, l, N_HEADS_A, HEAD_DIM_A),
                                  ka.reshape(b, l, N_KV_A, HEAD_DIM_A),
                                  va.reshape(b, l, N_KV_A, HEAD_DIM_A), pos_bias, sink)
    ya = ya.astype(x.dtype) @ w_branch_a

    qkv_b = jax.nn.silu(causal_depthwise_conv(qkv_b, conv_w))
    qb, kb, vb = jnp.split(qkv_b, (QK_B, 2 * QK_B), axis=-1)
    qb = l2norm(qb.reshape(b, l, N_HEADS_B, KEY_DIM_B))
    kb = l2norm(kb.reshape(b, l, N_HEADS_B, KEY_DIM_B))
    vb = vb.reshape(b, l, N_HEADS_B, VAL_DIM_B).astype(jnp.float32)
    beta = jax.nn.sigmoid(beta_raw.astype(jnp.float32))
    g = -jnp.exp(a_log.astype(jnp.float32)) * jax.nn.softplus(
        dt_raw.astype(jnp.float32) + dt_bias.astype(jnp.float32))
    o = gated_delta_rule(qb, kb, vb, g, beta)
    o = (o * lax.rsqrt(jnp.mean(o * o, -1, keepdims=True) + NORM_EPS)
         * dn_norm_g.astype(jnp.float32)
         * jax.nn.silu(z.reshape(b, l, N_HEADS_B, VAL_DIM_B).astype(jnp.float32)))
    yb = o.reshape(b, l, V_B).astype(x.dtype) @ w_branch_b

    gate_a, gate_b = jnp.split(jax.nn.sigmoid(gate_raw), 2, axis=-1)
    return (gate_a * ya + gate_b * yb) @ w_out


def setup_inputs(seed: int = 0) -> dict:
    key = jax.random.key(seed)
    ks = jax.random.split(key, 16)
    f32 = jnp.float32

    def nrm(k, shape, scale):
        return jax.random.normal(k, shape, f32) * scale

    dt = jnp.exp(jax.random.uniform(ks[9], (DEPTH, N_HEADS_B), f32, math.log(1e-3), math.log(1e-1)))
    return {
        "x": nrm(ks[0], (BATCH, SEQ, D_MODEL), 1.0),
        "rel_bias": nrm(ks[1], (NUM_BUCKETS, N_HEADS_A), 0.5),
        "ln_g": 1.0 + nrm(ks[2], (DEPTH, 3, D_MODEL), 0.05),
        "ln_b": nrm(ks[3], (DEPTH, 3, D_MODEL), 0.02),
        "ffn_w13": nrm(ks[4], (DEPTH, 2, D_MODEL, 2 * D_FF), D_MODEL ** -0.5),
        "ffn_w2": nrm(ks[5], (DEPTH, 2, D_FF, D_MODEL), DN_BETA * D_FF ** -0.5),
        "w_in": nrm(ks[6], (DEPTH, D_MODEL, N_IN), D_MODEL ** -0.5),
        "conv_w": nrm(ks[7], (DEPTH, CONV_K, CONV_CH), CONV_K ** -0.5),
        "a_log": jnp.log(jax.random.uniform(ks[8], (DEPTH, N_HEADS_B), f32, 1.0, 16.0)),
        "dt_bias": dt + jnp.log(-jnp.expm1(-dt)),
        "dn_norm_g": 1.0 + nrm(ks[10], (DEPTH, VAL_DIM_B), 0.05),
        "sinks": nrm(ks[11], (DEPTH, N_HEADS_A), 0.5),
        "w_branch_a": nrm(ks[12], (DEPTH, Q_A, D_MODEL), Q_A ** -0.5),
        "w_branch_b": nrm(ks[13], (DEPTH, V_B, D_MODEL), V_B ** -0.5),
        "w_out": nrm(ks[14], (DEPTH, D_MODEL, D_MODEL), DN_BETA * D_MODEL ** -0.5),
    }


def reference(x, rel_bias, ln_g, ln_b, ffn_w13, ffn_w2, w_in, conv_w, a_log, dt_bias,
              dn_norm_g, sinks, w_branch_a, w_branch_b, w_out):
    pos_bias = jnp.transpose(rel_bias[t5_bucket(band_rel())], (2, 0, 1)).astype(jnp.float32)
    for i in range(DEPTH):
        x = layernorm(DN_ALPHA * x + 0.5 * swiglu(x, ffn_w13[i, 0], ffn_w2[i, 0]), ln_g[i, 0], ln_b[i, 0])
        y = hybrid_mixer(x, w_in[i], conv_w[i], a_log[i], dt_bias[i], dn_norm_g[i], sinks[i],
                         w_branch_a[i], w_branch_b[i], w_out[i], pos_bias)
        x = layernorm(DN_ALPHA * x + y, ln_g[i, 1], ln_b[i, 1])
        x = layernorm(DN_ALPHA * x + 0.5 * swiglu(x, ffn_w13[i, 1], ffn_w2[i, 1]), ln_g[i, 2], ln_b[i, 2])
    return x
```

```python
import functools
import math

import jax
import jax.numpy as jnp
from jax import lax
from jax.experimental import pallas as pl
from jax.experimental.pallas import tpu as pltpu

BF16 = jnp.bfloat16
F32 = jnp.float32

N_HEADS_A = 16
N_KV_A = 4
HEAD_DIM_A = 64
WINDOW = 128
N_HEADS_B = 8
KEY_DIM_B = 128
VAL_DIM_B = 128
CONV_K = 4
NUM_BUCKETS = 32
MAX_DISTANCE = 128
LN_EPS = 1e-5
NORM_EPS = 1e-6
NEG_INF = -1e30

LANES = 128
SUBLANES = 8
VMEM_LIMIT_BYTES = 58 * 1024 * 1024

Q_A = N_HEADS_A * HEAD_DIM_A
KDUP = N_KV_A * 2 * HEAD_DIM_A
QK_B = N_HEADS_B * KEY_DIM_B
V_B = N_HEADS_B * VAL_DIM_B
COL_QA = 0
COL_KA = COL_QA + Q_A
COL_VA = COL_KA + KDUP
COL_QB = COL_VA + KDUP
COL_KB = COL_QB + QK_B
COL_VB = COL_KB + QK_B
COL_Z = COL_VB + V_B
COL_GA = COL_Z + V_B
N_MAIN = COL_GA + 2 * Q_A

DELTA_CHUNK = 128
FFN_CHUNK = 256
PROJ_CHUNK = 512


def _cparams(sem):
    return pltpu.CompilerParams(dimension_semantics=sem, vmem_limit_bytes=VMEM_LIMIT_BYTES)


def _resident(shape):
    nd = len(shape)
    return pl.BlockSpec(shape, lambda *_: (0,) * nd, pipeline_mode=pl.Buffered(1))


def _layernorm(y, g, b):
    mu = jnp.mean(y, axis=-1, keepdims=True)
    d = y - mu
    var = jnp.mean(d * d, axis=-1, keepdims=True)
    return d * lax.rsqrt(var + LN_EPS) * g + b


def _silu(x):
    return x * jax.nn.sigmoid(x)


def _ffn_ln_kernel(x_ref, wg_ref, wu_ref, w2_ref, g_ref, b_ref, o_ref, *, alpha):
    x = x_ref[...]
    xb = x.astype(BF16)
    acc = jnp.zeros(x.shape, F32)
    for c in range(wg_ref.shape[0]):
        h = jnp.dot(xb, wg_ref[c], preferred_element_type=F32)
        u = jnp.dot(xb, wu_ref[c], preferred_element_type=F32)
        a = (_silu(h) * u).astype(BF16)
        acc = acc + jnp.dot(a, w2_ref[c], preferred_element_type=F32)
    o_ref[...] = _layernorm(alpha * x + 0.5 * acc, g_ref[...], b_ref[...])


def _ffn_ln(x2, wg, wu, w2, g, b, *, alpha, tm):
    t, d = x2.shape
    return pl.pallas_call(
        functools.partial(_ffn_ln_kernel, alpha=alpha),
        grid=(t // tm,),
        in_specs=[pl.BlockSpec((tm, d), lambda i: (i, 0)),
                  _resident(wg.shape), _resident(wu.shape), _resident(w2.shape),
                  _resident(g.shape), _resident(b.shape)],
        out_specs=pl.BlockSpec((tm, d), lambda i: (i, 0)),
        out_shape=jax.ShapeDtypeStruct((t, d), F32),
        compiler_params=_cparams(("parallel",)),
    )(x2, wg, wu, w2, g, b)


def _in_proj_kernel(x_ref, w_ref, ws_ref, hc_ref, sm_ref):
    xb = x_ref[0].astype(BF16)
    for c in range(w_ref.shape[1] // PROJ_CHUNK):
        cols = slice(c * PROJ_CHUNK, (c + 1) * PROJ_CHUNK)
        hc_ref[0, :, cols] = jnp.dot(xb, w_ref[:, cols], preferred_element_type=F32).astype(BF16)
    small = jnp.dot(xb, ws_ref[...], preferred_element_type=F32)
    sm_ref[0] = small.T[:2 * N_HEADS_B, :]


def _in_proj(x3, w_main, w_small, *, tm):
    b, l, d = x3.shape
    return pl.pallas_call(
        _in_proj_kernel,
        grid=(b, l // tm),
        in_specs=[pl.BlockSpec((1, tm, d), lambda i, j: (i, j, 0)),
                  _resident(w_main.shape), _resident(w_small.shape)],
        out_specs=[pl.BlockSpec((1, tm, N_MAIN), lambda i, j: (i, j, 0)),
                   pl.BlockSpec((1, 2 * N_HEADS_B, tm), lambda i, j: (i, 0, j))],
        out_shape=[jax.ShapeDtypeStruct((b, l, N_MAIN), BF16),
                   jax.ShapeDtypeStruct((b, 2 * N_HEADS_B, l), F32)],
        compiler_params=_cparams(("parallel", "parallel")),
    )(x3, w_main, w_small)


def _t5_bucket(rel):
    n = jnp.maximum(rel, 0)
    max_exact = NUM_BUCKETS // 2
    nf = jnp.maximum(n, 1).astype(F32)
    large = max_exact + (jnp.log(nf / max_exact) / math.log(MAX_DISTANCE / max_exact)
                         * (NUM_BUCKETS - max_exact)).astype(jnp.int32)
    large = jnp.minimum(large, NUM_BUCKETS - 1)
    return jnp.where(n < max_exact, n, large)


def _pos_bias_kernel(rb_ref, bucket_ref, rel_ref, o_ref):
    h = pl.program_id(0)
    bucket = bucket_ref[...]
    rel = rel_ref[...]
    bias = jnp.zeros(bucket.shape, F32)
    for bk in range(NUM_BUCKETS):
        bias = jnp.where(bucket == bk, rb_ref[bk, h], bias)
    in_band = (rel >= 0) & (rel < WINDOW)
    own = lax.broadcasted_iota(jnp.int32, rel.shape, 1) >= WINDOW
    o_ref[0, 0] = jnp.where(in_band & own, bias, NEG_INF)
    o_ref[1, 0] = jnp.where(in_band, bias, NEG_INF)


def _pos_bias(rel_bias):
    r = jnp.arange(WINDOW, dtype=jnp.int32)[:, None]
    j = jnp.arange(2 * WINDOW, dtype=jnp.int32)[None, :]
    rel = r + WINDOW - j
    bucket = _t5_bucket(rel).astype(jnp.int32)
    full = pl.BlockSpec((WINDOW, 2 * WINDOW), lambda h: (0, 0))
    return pl.pallas_call(
        _pos_bias_kernel,
        grid=(N_HEADS_A,),
        in_specs=[pl.BlockSpec(memory_space=pltpu.SMEM), full, full],
        out_specs=pl.BlockSpec((2, 1, WINDOW, 2 * WINDOW), lambda h: (0, h, 0, 0)),
        out_shape=jax.ShapeDtypeStruct((2, N_HEADS_A, WINDOW, 2 * WINDOW), F32),
        compiler_params=_cparams(("arbitrary",)),
    )(rel_bias, bucket, rel)


def _attn_kernel(sink_ref, q_ref, kc_ref, kp_ref, vc_ref, vp_ref, bias_ref, o_ref):
    grp = N_HEADS_A // N_KV_A
    low = lax.broadcasted_iota(jnp.int32, (WINDOW, LANES), 1) < HEAD_DIM_A
    zero = jnp.zeros((WINDOW, LANES), BF16)
    scale = HEAD_DIM_A ** -0.5
    for g in range(N_KV_A):
        kcols = slice(g * LANES, (g + 1) * LANES)
        kk = jnp.concatenate([kp_ref[0, :, kcols], kc_ref[0, :, kcols]], axis=0)
        vv = jnp.concatenate([vp_ref[0, :, kcols], vc_ref[0, :, kcols]], axis=0)
        qs, sinks = [], []
        for pair in range(grp // 2):
            qp = q_ref[0, :, pl.ds((g * grp + 2 * pair) * HEAD_DIM_A, LANES)]
            qs += [jnp.where(low, qp, zero), jnp.where(low, zero, qp)]
        for h in range(grp):
            sinks.append(jnp.full((WINDOW, 1), sink_ref[g * grp + h], F32))
        q4 = jnp.concatenate(qs, axis=0)
        sk = jnp.concatenate(sinks, axis=0)
        s = lax.dot_general(q4, kk, (((1,), (1,)), ((), ())), preferred_element_type=F32)
        s = s * scale + bias_ref[0, g * grp:(g + 1) * grp].reshape(grp * WINDOW, 2 * WINDOW)
        m = jnp.maximum(jnp.max(s, axis=-1, keepdims=True), sk)
        p = jnp.exp(s - m)
        denom = jnp.sum(p, axis=-1, keepdims=True) + jnp.exp(sk - m)
        o = jnp.dot(p.astype(BF16), vv, preferred_element_type=F32) / denom
        for pair in range(grp // 2):
            r0 = (2 * pair) * WINDOW
            both = jnp.where(low, o[r0:r0 + WINDOW], o[r0 + WINDOW:r0 + 2 * WINDOW])
            o_ref[0, :, pl.ds((g * grp + 2 * pair) * HEAD_DIM_A, LANES)] = both.astype(BF16)


def _attention(hc, pos_bias, sink):
    b, l, _ = hc.shape
    nb = l // WINDOW
    kblk, vblk = COL_KA // KDUP, COL_VA // KDUP
    prev = lambda i, j: (i, jnp.maximum(j - 1, 0))
    return pl.pallas_call(
        _attn_kernel,
        grid=(b, nb),
        in_specs=[pl.BlockSpec(memory_space=pltpu.SMEM),
                  pl.BlockSpec((1, WINDOW, Q_A), lambda i, j: (i, j, COL_QA // Q_A)),
                  pl.BlockSpec((1, WINDOW, KDUP), lambda i, j: (i, j, kblk)),
                  pl.BlockSpec((1, WINDOW, KDUP), lambda i, j: (*prev(i, j), kblk)),
                  pl.BlockSpec((1, WINDOW, KDUP), lambda i, j: (i, j, vblk)),
                  pl.BlockSpec((1, WINDOW, KDUP), lambda i, j: (*prev(i, j), vblk)),
                  pl.BlockSpec((1, N_HEADS_A, WINDOW, 2 * WINDOW),
                               lambda i, j: (jnp.minimum(j, 1), 0, 0, 0))],
        out_specs=pl.BlockSpec((1, WINDOW, Q_A), lambda i, j: (i, j, 0)),
        out_shape=jax.ShapeDtypeStruct((b, l, Q_A), BF16),
        compiler_params=_cparams(("parallel", "arbitrary")),
    )(sink, hc, hc, hc, hc, hc, pos_bias)


def _chunk_scan(g, pos, reverse):
    n = g.shape[-1]
    s = 1
    while s < DELTA_CHUNK:
        if reverse:
            g = g + jnp.where(pos < DELTA_CHUNK - s, pltpu.roll(g, n - s, 1), 0.0)
        else:
            g = g + jnp.where(pos >= s, pltpu.roll(g, s, 1), 0.0)
        s *= 2
    return g


def _unit_lower_inverse(lm, row, col):
    c = lm.shape[0]
    eye = (row == col).astype(F32)
    mm = lambda a, b: jnp.dot(a.astype(BF16), b.astype(BF16), preferred_element_type=F32)
    base = SUBLANES
    same = lambda size: (row & -size) == (col & -size)
    ld = jnp.where(same(base), lm, 0.0)
    a1 = eye - ld
    ld2 = mm(ld, ld)
    ld4 = mm(ld2, ld2)
    p1 = a1 + mm(a1, ld2)
    x = p1 + mm(p1, ld4)
    blk = base
    while blk < c:
        e = jnp.where(same(2 * blk) & jnp.logical_not(same(blk)), lm, 0.0)
        x = x - mm(x, mm(e, x))
        blk *= 2
    return x


def _delta_kernel(q_ref, k_ref, v_ref, z_ref, sm_ref, cw_ref, alog_ref, dtb_ref, ng_ref, o_ref,
                  s_ref, xq_ref, xk_ref, xv_ref, *, n_chunks):
    c = DELTA_CHUNK
    blk = n_chunks * c
    halo = SUBLANES

    @pl.when(pl.program_id(1) == 0)
    def _():
        s_ref[...] = jnp.zeros(s_ref.shape, F32)
        for xr in (xq_ref, xk_ref, xv_ref):
            xr[0:halo, :] = jnp.zeros((halo, xr.shape[1]), F32)

    sm = sm_ref[0]
    pos = lax.broadcasted_iota(jnp.int32, (N_HEADS_B, blk), 1) & (c - 1)
    beta = jax.nn.sigmoid(sm[0:N_HEADS_B])
    glog = -jnp.exp(alog_ref[...]) * jax.nn.softplus(sm[N_HEADS_B:2 * N_HEADS_B] + dtb_ref[...])
    gc = _chunk_scan(glog, pos, reverse=False)
    tail = _chunk_scan(glog, pos, reverse=True) - glog
    rows = jnp.concatenate([beta, gc, jnp.exp(gc), jnp.exp(tail), jnp.exp(gc + tail),
                            jnp.zeros((LANES - 5 * N_HEADS_B, blk), F32)], axis=0)
    cols = rows.T

    row_i = lax.broadcasted_iota(jnp.int32, (c, c), 0)
    col_i = lax.broadcasted_iota(jnp.int32, (c, c), 1)

    for xr, r in ((xq_ref, q_ref), (xk_ref, k_ref), (xv_ref, v_ref)):
        xr[halo:halo + blk, :] = r[0].astype(F32)

    def conv_silu(xr, kind, lanes):
        acc = None
        for j in range(CONV_K):
            w = cw_ref[j:j + 1, pl.ds(kind * QK_B + lanes.start, LANES)]
            term = xr[halo - (CONV_K - 1) + j: halo - (CONV_K - 1) + j + blk, lanes] * w
            acc = term if acc is None else acc + term
        return _silu(acc)

    def l2norm(t):
        return t * lax.rsqrt(jnp.sum(t * t, axis=-1, keepdims=True) + NORM_EPS)

    mm = lambda a, b: jnp.dot(a.astype(BF16), b.astype(BF16), preferred_element_type=F32)

    for h in range(N_HEADS_B):
        lanes = slice(h * LANES, (h + 1) * LANES)
        qh = l2norm(conv_silu(xq_ref, 0, lanes)) * (KEY_DIM_B ** -0.5)
        kh = l2norm(conv_silu(xk_ref, 1, lanes))
        vh = conv_silu(xv_ref, 2, lanes)
        state = s_ref[h]
        for ci in range(n_chunks):
            tok = slice(ci * c, (ci + 1) * c)
            qc, kc, vc = qh[tok], kh[tok], vh[tok]
            beta_c = cols[tok, h:h + 1]
            gc_c = cols[tok, N_HEADS_B + h:N_HEADS_B + h + 1]
            egc_c = cols[tok, 2 * N_HEADS_B + h:2 * N_HEADS_B + h + 1]
            etail_c = cols[tok, 3 * N_HEADS_B + h:3 * N_HEADS_B + h + 1]
            egl = cols[ci * c:ci * c + 1, 4 * N_HEADS_B + h:4 * N_HEADS_B + h + 1]
            gc_r = gc[h:h + 1, tok]

            kq = jnp.concatenate([kc, qc], axis=0).astype(BF16)
            prods = lax.dot_general(kq, kc.astype(BF16), (((1,), (1,)), ((), ())),
                                    preferred_element_type=F32)
            decay = jnp.exp(jnp.where(row_i >= col_i, gc_c - gc_r, NEG_INF))
            lm = jnp.where(row_i > col_i, beta_c * prods[:c] * decay, 0.0)
            a_intra = prods[c:] * decay
            t_inv = _unit_lower_inverse(lm, row_i, col_i)

            ks_qs = mm(jnp.concatenate([kc, qc * egc_c], axis=0), state)
            rhs = beta_c * (vc - egc_c * ks_qs[:c])
            v_new = mm(t_inv, rhs)
            o = ks_qs[c:] + mm(a_intra, v_new)
            state = state * egl + mm((kc * etail_c).T, v_new)

            zc = z_ref[0, tok, lanes].astype(F32)
            o = o * lax.rsqrt(jnp.mean(o * o, axis=-1, keepdims=True) + NORM_EPS) * ng_ref[...] * _silu(zc)
            o_ref[0, tok, lanes] = o.astype(BF16)
        s_ref[h] = state

    for xr in (xq_ref, xk_ref, xv_ref):
        xr[0:halo, :] = xr[blk:blk + halo, :]


def _delta(hc, small, conv_w, a_log, dt_bias, norm_g, *, n_chunks):
    b, l, _ = hc.shape
    blk = n_chunks * DELTA_CHUNK
    col = lambda cidx: pl.BlockSpec((1, blk, QK_B), lambda i, j: (i, j, cidx))
    xbuf = pltpu.VMEM((blk + SUBLANES, QK_B), F32)
    return pl.pallas_call(
        functools.partial(_delta_kernel, n_chunks=n_chunks),
        grid=(b, l // blk),
        in_specs=[col(COL_QB // QK_B), col(COL_KB // QK_B), col(COL_VB // QK_B), col(COL_Z // QK_B),
                  pl.BlockSpec((1, 2 * N_HEADS_B, blk), lambda i, j: (i, 0, j)),
                  _resident(conv_w.shape), _resident(a_log.shape), _resident(dt_bias.shape),
                  _resident(norm_g.shape)],
        out_specs=pl.BlockSpec((1, blk, V_B), lambda i, j: (i, j, 0)),
        out_shape=jax.ShapeDtypeStruct((b, l, V_B), BF16),
        scratch_shapes=[pltpu.VMEM((N_HEADS_B, KEY_DIM_B, VAL_DIM_B), F32), xbuf, xbuf, xbuf],
        compiler_params=_cparams(("parallel", "arbitrary")),
    )(hc, hc, hc, hc, small, conv_w, a_log, dt_bias, norm_g)


def _merge_kernel(x_ref, ya_ref, yb_ref, ga_ref, gb_ref, wa_ref, wb_ref, wo_ref, g_ref, b_ref, o_ref, *, alpha):
    ya = jnp.dot(ya_ref[...], wa_ref[...], preferred_element_type=F32)
    yb = jnp.dot(yb_ref[...], wb_ref[...], preferred_element_type=F32)
    mix = (jax.nn.sigmoid(ga_ref[...].astype(F32)) * ya + jax.nn.sigmoid(gb_ref[...].astype(F32)) * yb)
    y = jnp.dot(mix.astype(BF16), wo_ref[...], preferred_element_type=F32)
    o_ref[...] = _layernorm(alpha * x_ref[...] + y, g_ref[...], b_ref[...])


def _merge(x2, ya, yb, hc2, wa, wb, wo, g, b, *, alpha, tm):
    t, d = x2.shape
    tile = lambda cidx=0: pl.BlockSpec((tm, d), lambda i: (i, cidx))
    return pl.pallas_call(
        functools.partial(_merge_kernel, alpha=alpha),
        grid=(t // tm,),
        in_specs=[tile(), tile(), tile(), tile(COL_GA // Q_A), tile(COL_GA // Q_A + 1),
                  _resident(wa.shape), _resident(wb.shape), _resident(wo.shape),
                  _resident(g.shape), _resident(b.shape)],
        out_specs=tile(),
        out_shape=jax.ShapeDtypeStruct((t, d), F32),
        compiler_params=_cparams(("parallel",)),
    )(x2, ya, yb, hc2, hc2, wa, wb, wo, g, b)


def _prep_w_in(w_in):
    d = w_in.shape[0]
    o = 0
    parts = {}
    for name, size in (("qa", Q_A), ("ka", N_KV_A * HEAD_DIM_A), ("va", N_KV_A * HEAD_DIM_A),
                       ("qkv", 2 * QK_B + V_B), ("beta", N_HEADS_B), ("dt", N_HEADS_B),
                       ("z", V_B), ("gate", 2 * Q_A)):
        parts[name] = w_in[:, o:o + size]
        o += size

    def dup(w):
        w = w.reshape(d, N_KV_A, 1, HEAD_DIM_A)
        return jnp.broadcast_to(w, (d, N_KV_A, 2, HEAD_DIM_A)).reshape(d, KDUP)

    w_main = jnp.concatenate([parts["qa"], dup(parts["ka"]), dup(parts["va"]), parts["qkv"],
                              parts["z"], parts["gate"]], axis=1).astype(BF16)
    w_small = jnp.concatenate([parts["beta"], parts["dt"],
                               jnp.zeros((d, LANES - 2 * N_HEADS_B), w_in.dtype)], axis=1).astype(BF16)
    return w_main, w_small


def _prep_ffn(w13, w2):
    d, f2 = w13.shape
    f = f2 // 2
    nc = f // FFN_CHUNK
    wg = w13[:, :f].reshape(d, nc, FFN_CHUNK).transpose(1, 0, 2).astype(BF16)
    wu = w13[:, f:].reshape(d, nc, FFN_CHUNK).transpose(1, 0, 2).astype(BF16)
    return wg, wu, w2.reshape(nc, FFN_CHUNK, d).astype(BF16)


def kernel(x, rel_bias, ln_g, ln_b, ffn_w13, ffn_w2, w_in, conv_w, a_log, dt_bias, dn_norm_g, sinks,
           w_branch_a, w_branch_b, w_out):
    b, l, d = x.shape
    depth = w_in.shape[0]
    alpha = (2 * depth) ** 0.25
    t = b * l
    tm = 512
    pos_bias = _pos_bias(rel_bias)
    x2 = x.reshape(t, d)
    row = lambda v: v.reshape(1, -1)
    colv = lambda v: v.reshape(-1, 1)
    for i in range(depth):
        x2 = _ffn_ln(x2, *_prep_ffn(ffn_w13[i, 0], ffn_w2[i, 0]), row(ln_g[i, 0]), row(ln_b[i, 0]),
                     alpha=alpha, tm=tm)
        w_main, w_small = _prep_w_in(w_in[i])
        hc, small = _in_proj(x2.reshape(b, l, d), w_main, w_small, tm=tm)
        ya = _attention(hc, pos_bias, sinks[i])
        yb = _delta(hc, small, conv_w[i], colv(a_log[i]), colv(dt_bias[i]), row(dn_norm_g[i]), n_chunks=2)
        x2 = _merge(x2, ya.reshape(t, Q_A), yb.reshape(t, V_B), hc.reshape(t, N_MAIN),
                    w_branch_a[i].astype(BF16), w_branch_b[i].astype(BF16), w_out[i].astype(BF16),
                    row(ln_g[i, 1]), row(ln_b[i, 1]), alpha=alpha, tm=tm)
        x2 = _ffn_ln(x2, *_prep_ffn(ffn_w13[i, 1], ffn_w2[i, 1]), row(ln_g[i, 2]), row(ln_b[i, 2]),
                     alpha=alpha, tm=tm)
    return x2.reshape(b, l, d)
```

```python
import functools
import math

import jax
import jax.numpy as jnp
from jax import lax
from jax.experimental import pallas as pl
from jax.experimental.pallas import tpu as pltpu

BF16 = jnp.bfloat16
F32 = jnp.float32

N_HEADS_A = 16
N_KV_A = 4
HEAD_DIM_A = 64
WINDOW = 128
N_HEADS_B = 8
KEY_DIM_B = 128
VAL_DIM_B = 128
CONV_K = 4
NUM_BUCKETS = 32
MAX_DISTANCE = 128
LN_EPS = 1e-5
NORM_EPS = 1e-6
NEG_INF = -1e30

LANES = 128
SUBLANES = 8
VMEM_LIMIT_BYTES = 58 * 1024 * 1024

Q_A = N_HEADS_A * HEAD_DIM_A
KDUP = N_KV_A * 2 * HEAD_DIM_A
QK_B = N_HEADS_B * KEY_DIM_B
V_B = N_HEADS_B * VAL_DIM_B
COL_QA = 0
COL_KA = COL_QA + Q_A
COL_VA = COL_KA + KDUP
COL_QB = COL_VA + KDUP
COL_KB = COL_QB + QK_B
COL_VB = COL_KB + QK_B
COL_Z = COL_VB + V_B
COL_GA = COL_Z + V_B
N_MAIN = COL_GA + 2 * Q_A

DELTA_CHUNK = 128
FFN_CHUNK = 256
PROJ_CHUNK = 512


def _cparams(sem):
    return pltpu.CompilerParams(dimension_semantics=sem, vmem_limit_bytes=VMEM_LIMIT_BYTES)


def _resident(shape):
    nd = len(shape)
    return pl.BlockSpec(shape, lambda *_: (0,) * nd, pipeline_mode=pl.Buffered(1))


def _layernorm(y, g, b):
    mu = jnp.mean(y, axis=-1, keepdims=True)
    d = y - mu
    var = jnp.mean(d * d, axis=-1, keepdims=True)
    return d * lax.rsqrt(var + LN_EPS) * g + b


def _silu(x):
    return x * jax.nn.sigmoid(x)


def _ffn_ln_kernel(x_ref, wg_ref, wu_ref, w2_ref, g_ref, b_ref, o_ref, *, alpha):
    x = x_ref[...]
    xb = x.astype(BF16)
    acc = jnp.zeros(x.shape, F32)
    for c in range(wg_ref.shape[0]):
        h = jnp.dot(xb, wg_ref[c], preferred_element_type=F32)
        u = jnp.dot(xb, wu_ref[c], preferred_element_type=F32)
        a = (_silu(h) * u).astype(BF16)
        acc = acc + jnp.dot(a, w2_ref[c], preferred_element_type=F32)
    o_ref[...] = _layernorm(alpha * x + 0.5 * acc, g_ref[...], b_ref[...])


def _ffn_ln(x2, wg, wu, w2, g, b, *, alpha, tm):
    t, d = x2.shape
    return pl.pallas_call(
        functools.partial(_ffn_ln_kernel, alpha=alpha),
        grid=(t // tm,),
        in_specs=[pl.BlockSpec((tm, d), lambda i: (i, 0)),
                  _resident(wg.shape), _resident(wu.shape), _resident(w2.shape),
                  _resident(g.shape), _resident(b.shape)],
        out_specs=pl.BlockSpec((tm, d), lambda i: (i, 0)),
        out_shape=jax.ShapeDtypeStruct((t, d), F32),
        compiler_params=_cparams(("parallel",)),
    )(x2, wg, wu, w2, g, b)


def _in_proj_kernel(x_ref, w_ref, ws_ref, hc_ref, sm_ref):
    xb = x_ref[0].astype(BF16)
    for c in range(w_ref.shape[1] // PROJ_CHUNK):
        cols = slice(c * PROJ_CHUNK, (c + 1) * PROJ_CHUNK)
        hc_ref[0, :, cols] = jnp.dot(xb, w_ref[:, cols], preferred_element_type=F32).astype(BF16)
    small = jnp.dot(xb, ws_ref[...], preferred_element_type=F32)
    sm_ref[0] = small.T[:2 * N_HEADS_B, :]


def _in_proj(x3, w_main, w_small, *, tm):
    b, l, d = x3.shape
    return pl.pallas_call(
        _in_proj_kernel,
        grid=(b, l // tm),
        in_specs=[pl.BlockSpec((1, tm, d), lambda i, j: (i, j, 0)),
                  _resident(w_main.shape), _resident(w_small.shape)],
        out_specs=[pl.BlockSpec((1, tm, N_MAIN), lambda i, j: (i, j, 0)),
                   pl.BlockSpec((1, 2 * N_HEADS_B, tm), lambda i, j: (i, 0, j))],
        out_shape=[jax.ShapeDtypeStruct((b, l, N_MAIN), BF16),
                   jax.ShapeDtypeStruct((b, 2 * N_HEADS_B, l), F32)],
        compiler_params=_cparams(("parallel", "parallel")),
    )(x3, w_main, w_small)


def _t5_bucket(rel):
    n = jnp.maximum(rel, 0)
    max_exact = NUM_BUCKETS // 2
    nf = jnp.maximum(n, 1).astype(F32)
    large = max_exact + (jnp.log(nf / max_exact) / math.log(MAX_DISTANCE / max_exact)
                         * (NUM_BUCKETS - max_exact)).astype(jnp.int32)
    large = jnp.minimum(large, NUM_BUCKETS - 1)
    return jnp.where(n < max_exact, n, large)


def _pos_bias_kernel(rb_ref, bucket_ref, rel_ref, o_ref):
    h = pl.program_id(0)
    bucket = bucket_ref[...]
    rel = rel_ref[...]
    bias = jnp.zeros(bucket.shape, F32)
    for bk in range(NUM_BUCKETS):
        bias = jnp.where(bucket == bk, rb_ref[bk, h], bias)
    in_band = (rel >= 0) & (rel < WINDOW)
    own = lax.broadcasted_iota(jnp.int32, rel.shape, 1) >= WINDOW
    o_ref[0, 0] = jnp.where(in_band & own, bias, NEG_INF)
    o_ref[1, 0] = jnp.where(in_band, bias, NEG_INF)


def _pos_bias(rel_bias):
    r = jnp.arange(WINDOW, dtype=jnp.int32)[:, None]
    j = jnp.arange(2 * WINDOW, dtype=jnp.int32)[None, :]
    rel = r + WINDOW - j
    bucket = _t5_bucket(rel).astype(jnp.int32)
    full = pl.BlockSpec((WINDOW, 2 * WINDOW), lambda h: (0, 0))
    return pl.pallas_call(
        _pos_bias_kernel,
        grid=(N_HEADS_A,),
        in_specs=[pl.BlockSpec(memory_space=pltpu.SMEM), full, full],
        out_specs=pl.BlockSpec((2, 1, WINDOW, 2 * WINDOW), lambda h: (0, h, 0, 0)),
        out_shape=jax.ShapeDtypeStruct((2, N_HEADS_A, WINDOW, 2 * WINDOW), F32),
        compiler_params=_cparams(("arbitrary",)),
    )(rel_bias, bucket, rel)


def _attn_kernel(sink_ref, q_ref, kc_ref, kp_ref, vc_ref, vp_ref, bias_ref, o_ref):
    grp = N_HEADS_A // N_KV_A
    low = lax.broadcasted_iota(jnp.int32, (WINDOW, LANES), 1) < HEAD_DIM_A
    zero = jnp.zeros((WINDOW, LANES), BF16)
    scale = HEAD_DIM_A ** -0.5
    for g in range(N_KV_A):
        kcols = slice(g * LANES, (g + 1) * LANES)
        kk = jnp.concatenate([kp_ref[0, :, kcols], kc_ref[0, :, kcols]], axis=0)
        vv = jnp.concatenate([vp_ref[0, :, kcols], vc_ref[0, :, kcols]], axis=0)
        qs, sinks = [], []
        for pair in range(grp // 2):
            qp = q_ref[0, :, pl.ds((g * grp + 2 * pair) * HEAD_DIM_A, LANES)]
            qs += [jnp.where(low, qp, zero), jnp.where(low, zero, qp)]
        for h in range(grp):
            sinks.append(jnp.full((WINDOW, 1), sink_ref[g * grp + h], F32))
        q4 = jnp.concatenate(qs, axis=0)
        sk = jnp.concatenate(sinks, axis=0)
        s = lax.dot_general(q4, kk, (((1,), (1,)), ((), ())), preferred_element_type=F32)
        s = s * scale + bias_ref[0, g * grp:(g + 1) * grp].reshape(grp * WINDOW, 2 * WINDOW)
        m = jnp.maximum(jnp.max(s, axis=-1, keepdims=True), sk)
        p = jnp.exp(s - m)
        denom = jnp.sum(p, axis=-1, keepdims=True) + jnp.exp(sk - m)
        o = jnp.dot(p.astype(BF16), vv, preferred_element_type=F32) / denom
        for pair in range(grp // 2):
            r0 = (2 * pair) * WINDOW
            both = jnp.where(low, o[r0:r0 + WINDOW], o[r0 + WINDOW:r0 + 2 * WINDOW])
            o_ref[0, :, pl.ds((g * grp + 2 * pair) * HEAD_DIM_A, LANES)] = both.astype(BF16)


def _attention(hc, pos_bias, sink):
    b, l, _ = hc.shape
    nb = l // WINDOW
    kblk, vblk = COL_KA // KDUP, COL_VA // KDUP
    prev = lambda i, j: (i, jnp.maximum(j - 1, 0))
    return pl.pallas_call(
        _attn_kernel,
        grid=(b, nb),
        in_specs=[pl.BlockSpec(memory_space=pltpu.SMEM),
                  pl.BlockSpec((1, WINDOW, Q_A), lambda i, j: (i, j, COL_QA // Q_A)),
                  pl.BlockSpec((1, WINDOW, KDUP), lambda i, j: (i, j, kblk)),
                  pl.BlockSpec((1, WINDOW, KDUP), lambda i, j: (*prev(i, j), kblk)),
                  pl.BlockSpec((1, WINDOW, KDUP), lambda i, j: (i, j, vblk)),
                  pl.BlockSpec((1, WINDOW, KDUP), lambda i, j: (*prev(i, j), vblk)),
                  pl.BlockSpec((1, N_HEADS_A, WINDOW, 2 * WINDOW),
                               lambda i, j: (jnp.minimum(j, 1), 0, 0, 0))],
        out_specs=pl.BlockSpec((1, WINDOW, Q_A), lambda i, j: (i, j, 0)),
        out_shape=jax.ShapeDtypeStruct((b, l, Q_A), BF16),
        compiler_params=_cparams(("parallel", "arbitrary")),
    )(sink, hc, hc, hc, hc, hc, pos_bias)


def _chunk_scan(g, pos, reverse):
    n = g.shape[-1]
    s = 1
    while s < DELTA_CHUNK:
        if reverse:
            g = g + jnp.where(pos < DELTA_CHUNK - s, pltpu.roll(g, n - s, 1), 0.0)
        else:
            g = g + jnp.where(pos >= s, pltpu.roll(g, s, 1), 0.0)
        s *= 2
    return g


def _halves(t):
    return t[:, :LANES], t[:, LANES:]


def _pair(a, b):
    return jnp.concatenate([a, b], axis=1)


def _block_diag(t):
    a, b = _halves(t)
    z = jnp.zeros_like(a)
    return jnp.concatenate([_pair(a, z), _pair(z, b)], axis=0)


def _mm2(a, b):
    return jnp.dot(a.astype(BF16), _block_diag(b.astype(BF16)), preferred_element_type=F32)


def _unit_lower_inverses(lms, row, col):
    c = row.shape[0]
    eye = (row == col).astype(F32)
    base = SUBLANES
    same = lambda size: (row & -size) == (col & -size)
    lds = [jnp.where(same(base), lm, 0.0) for lm in lms]
    a1s = [eye - ld for ld in lds]
    ldbs = [ld.astype(BF16) for ld in lds]
    ld2bs = [_mm2(ldb, ldb).astype(BF16) for ldb in ldbs]
    ld4s = [_mm2(ld2b, ld2b) for ld2b in ld2bs]
    p1s = [a1 + _mm2(a1, ld2b) for a1, ld2b in zip(a1s, ld2bs)]
    xs = [p1 + _mm2(p1, ld4) for p1, ld4 in zip(p1s, ld4s)]
    lmbs = [lm.astype(BF16) for lm in lms]
    zero = jnp.zeros(row.shape, BF16)
    blk = base
    while blk < c:
        mask = same(2 * blk) & jnp.logical_not(same(blk))
        xbs = [x.astype(BF16) for x in xs]
        exs = [_mm2(jnp.where(mask, lmb, zero), xb) for lmb, xb in zip(lmbs, xbs)]
        xs = [x - _mm2(xb, ex) for x, xb, ex in zip(xs, xbs, exs)]
        blk *= 2
    return xs


def _delta_kernel(q_ref, k_ref, v_ref, z_ref, sm_ref, cw_ref, alog_ref, dtb_ref, ng_ref, o_ref,
                  s_ref, xq_ref, xk_ref, xv_ref, *, n_chunks):
    c = DELTA_CHUNK
    blk = n_chunks * c
    halo = SUBLANES

    @pl.when(pl.program_id(1) == 0)
    def _():
        s_ref[...] = jnp.zeros(s_ref.shape, F32)
        for xr in (xq_ref, xk_ref, xv_ref):
            xr[:, 0:halo, :] = jnp.zeros((xr.shape[0], halo, LANES), F32)

    sm = sm_ref[0]
    pos = lax.broadcasted_iota(jnp.int32, (N_HEADS_B, blk), 1) & (c - 1)
    beta = jax.nn.sigmoid(sm[0:N_HEADS_B])
    glog = -jnp.exp(alog_ref[...]) * jax.nn.softplus(sm[N_HEADS_B:2 * N_HEADS_B] + dtb_ref[...])
    gc = _chunk_scan(glog, pos, reverse=False)
    tail = _chunk_scan(glog, pos, reverse=True) - glog
    etail = jnp.exp(tail)
    rows = jnp.concatenate([beta, gc, jnp.exp(gc), jnp.exp(gc + tail),
                            jnp.zeros((LANES - 4 * N_HEADS_B, blk), F32)], axis=0)
    cols = rows.T

    n_pairs = N_HEADS_B // 2
    wide = 2 * LANES
    row_i = lax.broadcasted_iota(jnp.int32, (c, wide), 0)
    col_i = lax.broadcasted_iota(jnp.int32, (c, wide), 1) & (LANES - 1)
    causal = row_i >= col_i
    strict = row_i > col_i

    for xr, r in ((xq_ref, q_ref), (xk_ref, k_ref), (xv_ref, v_ref)):
        for h in range(N_HEADS_B):
            xr[h, halo:halo + blk, :] = r[0, :, h * LANES:(h + 1) * LANES].astype(F32)

    def conv_silu(xr, kind, h):
        acc = None
        for j in range(CONV_K):
            w = cw_ref[j:j + 1, pl.ds(kind * QK_B + h * LANES, LANES)]
            term = xr[h, halo - (CONV_K - 1) + j: halo - (CONV_K - 1) + j + blk, :] * w
            acc = term if acc is None else acc + term
        return _silu(acc)

    def l2norm(t):
        return t * lax.rsqrt(jnp.sum(t * t, axis=-1, keepdims=True) + NORM_EPS)

    def col2(kind, p, rows_):
        base = kind * N_HEADS_B + 2 * p
        return _pair(jnp.broadcast_to(cols[rows_, base:base + 1], (rows_.stop - rows_.start, LANES)),
                     jnp.broadcast_to(cols[rows_, base + 1:base + 2], (rows_.stop - rows_.start, LANES)))

    def row2(arr, p, t):
        return _pair(arr[2 * p:2 * p + 1, t], arr[2 * p + 1:2 * p + 2, t])

    q_all = [l2norm(conv_silu(xq_ref, 0, h)) * (KEY_DIM_B ** -0.5) for h in range(N_HEADS_B)]
    k_all = [l2norm(conv_silu(xk_ref, 1, h)) for h in range(N_HEADS_B)]
    v_all = [conv_silu(xv_ref, 2, h) for h in range(N_HEADS_B)]

    items = [(p, ci) for ci in range(n_chunks) for p in range(n_pairs)]
    toks = [slice(ci * c, (ci + 1) * c) for _, ci in items]
    pt = list(zip([p for p, _ in items], toks))
    qs = [_pair(q_all[2 * p][t], q_all[2 * p + 1][t]) for p, t in pt]
    ks = [_pair(k_all[2 * p][t], k_all[2 * p + 1][t]) for p, t in pt]
    vs = [_pair(v_all[2 * p][t], v_all[2 * p + 1][t]) for p, t in pt]
    kts = [_pair(k_all[2 * p][t].T, k_all[2 * p + 1][t].T) for p, t in pt]
    betas = [col2(0, p, t) for p, t in pt]
    gccs = [col2(1, p, t) for p, t in pt]
    egcs = [col2(2, p, t) for p, t in pt]
    gcrs = [row2(gc, p, t) for p, t in pt]
    etrs = [row2(etail, p, t) for p, t in pt]

    prods = [jnp.dot(jnp.concatenate([k, q], axis=0).astype(BF16), _block_diag(kt.astype(BF16)),
                     preferred_element_type=F32) for k, q, kt in zip(ks, qs, kts)]
    decays = [jnp.exp(jnp.where(causal, gcc - gcr, NEG_INF)) for gcc, gcr in zip(gccs, gcrs)]
    lms = [jnp.where(strict, beta * pr[:c] * d, 0.0) for beta, pr, d in zip(betas, prods, decays)]
    a_intras = [pr[c:] * d for pr, d in zip(prods, decays)]
    t_invs = _unit_lower_inverses(lms, row_i, col_i)
    uws = []
    for t_inv, k, v, beta, egc in zip(t_invs, ks, vs, betas, egcs):
        (vb0, vb1), (kb0, kb1) = _halves((v * beta).astype(BF16)), _halves((k * (beta * egc)).astype(BF16))
        z = jnp.zeros_like(vb0)
        rhs = jnp.concatenate([jnp.concatenate([vb0, kb0, z, z], axis=1),
                               jnp.concatenate([z, z, vb1, kb1], axis=1)], axis=0)
        uws.append(jnp.dot(t_inv.astype(BF16), rhs, preferred_element_type=F32))
    us = [_pair(uw[:, 0:LANES], uw[:, 2 * LANES:3 * LANES]) for uw in uws]
    wqs = [jnp.concatenate([_pair(uw[:, LANES:2 * LANES], uw[:, 3 * LANES:]), q * egc], axis=0).astype(BF16)
           for uw, q, egc in zip(uws, qs, egcs)]
    akts = [jnp.concatenate([a, kt * etr], axis=0).astype(BF16)
            for a, kt, etr in zip(a_intras, kts, etrs)]

    states = [s_ref[p] for p in range(n_pairs)]
    for ci in range(n_chunks):
        idx = [ci * n_pairs + p for p in range(n_pairs)]
        tok = slice(ci * c, (ci + 1) * c)
        r1 = [jnp.dot(wqs[i], _block_diag(states[p].astype(BF16)), preferred_element_type=F32)
              for p, i in enumerate(idx)]
        v_news = [us[i] - r[:c] for i, r in zip(idx, r1)]
        r2 = [jnp.dot(akts[i], _block_diag(vn.astype(BF16)), preferred_element_type=F32)
              for i, vn in zip(idx, v_news)]
        for p in range(n_pairs):
            base = 3 * N_HEADS_B + 2 * p
            egl = _pair(jnp.broadcast_to(cols[ci * c:ci * c + 1, base:base + 1], (1, LANES)),
                        jnp.broadcast_to(cols[ci * c:ci * c + 1, base + 1:base + 2], (1, LANES)))
            states[p] = states[p] * egl + r2[p][c:]
            o = r1[p][c:] + r2[p][:c]
            o = _pair(*[u * lax.rsqrt(jnp.mean(u * u, axis=-1, keepdims=True) + NORM_EPS) * ng_ref[...]
                        for u in _halves(o)])
            lanes = slice(p * wide, (p + 1) * wide)
            o_ref[0, tok, lanes] = (o * _silu(z_ref[0, tok, lanes].astype(F32))).astype(BF16)
    for p in range(n_pairs):
        s_ref[p] = states[p]

    for xr in (xq_ref, xk_ref, xv_ref):
        xr[:, 0:halo, :] = xr[:, blk:blk + halo, :]


def _delta(hc, small, conv_w, a_log, dt_bias, norm_g, *, n_chunks):
    b, l, _ = hc.shape
    blk = n_chunks * DELTA_CHUNK
    col = lambda cidx: pl.BlockSpec((1, blk, QK_B), lambda i, j: (i, j, cidx))
    xbuf = pltpu.VMEM((N_HEADS_B, blk + SUBLANES, LANES), F32)
    return pl.pallas_call(
        functools.partial(_delta_kernel, n_chunks=n_chunks),
        grid=(b, l // blk),
        in_specs=[col(COL_QB // QK_B), col(COL_KB // QK_B), col(COL_VB // QK_B), col(COL_Z // QK_B),
                  pl.BlockSpec((1, 2 * N_HEADS_B, blk), lambda i, j: (i, 0, j)),
                  _resident(conv_w.shape), _resident(a_log.shape), _resident(dt_bias.shape),
                  _resident(norm_g.shape)],
        out_specs=pl.BlockSpec((1, blk, V_B), lambda i, j: (i, j, 0)),
        out_shape=jax.ShapeDtypeStruct((b, l, V_B), BF16),
        scratch_shapes=[pltpu.VMEM((N_HEADS_B // 2, KEY_DIM_B, 2 * VAL_DIM_B), F32), xbuf, xbuf, xbuf],
        compiler_params=_cparams(("parallel", "arbitrary")),
    )(hc, hc, hc, hc, small, conv_w, a_log, dt_bias, norm_g)


def _merge_kernel(x_ref, ya_ref, yb_ref, ga_ref, gb_ref, wa_ref, wb_ref, wo_ref, g_ref, b_ref, o_ref, *, alpha):
    ya = jnp.dot(ya_ref[...], wa_ref[...], preferred_element_type=F32)
    yb = jnp.dot(yb_ref[...], wb_ref[...], preferred_element_type=F32)
    mix = (jax.nn.sigmoid(ga_ref[...].astype(F32)) * ya + jax.nn.sigmoid(gb_ref[...].astype(F32)) * yb)
    y = jnp.dot(mix.astype(BF16), wo_ref[...], preferred_element_type=F32)
    o_ref[...] = _layernorm(alpha * x_ref[...] + y, g_ref[...], b_ref[...])


def _merge(x2, ya, yb, hc2, wa, wb, wo, g, b, *, alpha, tm):
    t, d = x2.shape
    tile = lambda cidx=0: pl.BlockSpec((tm, d), lambda i: (i, cidx))
    return pl.pallas_call(
        functools.partial(_merge_kernel, alpha=alpha),
        grid=(t // tm,),
        in_specs=[tile(), tile(), tile(), tile(COL_GA // Q_A), tile(COL_GA // Q_A + 1),
                  _resident(wa.shape), _resident(wb.shape), _resident(wo.shape),
                  _resident(g.shape), _resident(b.shape)],
        out_specs=tile(),
        out_shape=jax.ShapeDtypeStruct((t, d), F32),
        compiler_params=_cparams(("parallel",)),
    )(x2, ya, yb, hc2, hc2, wa, wb, wo, g, b)


def _prep_w_in(w_in):
    d = w_in.shape[0]
    o = 0
    parts = {}
    for name, size in (("qa", Q_A), ("ka", N_KV_A * HEAD_DIM_A), ("va", N_KV_A * HEAD_DIM_A),
                       ("qkv", 2 * QK_B + V_B), ("beta", N_HEADS_B), ("dt", N_HEADS_B),
                       ("z", V_B), ("gate", 2 * Q_A)):
        parts[name] = w_in[:, o:o + size]
        o += size

    def dup(w):
        w = w.reshape(d, N_KV_A, 1, HEAD_DIM_A)
        return jnp.broadcast_to(w, (d, N_KV_A, 2, HEAD_DIM_A)).reshape(d, KDUP)

    w_main = jnp.concatenate([parts["qa"], dup(parts["ka"]), dup(parts["va"]), parts["qkv"],
                              parts["z"], parts["gate"]], axis=1).astype(BF16)
    w_small = jnp.concatenate([parts["beta"], parts["dt"],
                               jnp.zeros((d, LANES - 2 * N_HEADS_B), w_in.dtype)], axis=1).astype(BF16)
    return w_main, w_small


def _prep_ffn(w13, w2):
    d, f2 = w13.shape
    f = f2 // 2
    nc = f // FFN_CHUNK
    wg = w13[:, :f].reshape(d, nc, FFN_CHUNK).transpose(1, 0, 2).astype(BF16)
    wu = w13[:, f:].reshape(d, nc, FFN_CHUNK).transpose(1, 0, 2).astype(BF16)
    return wg, wu, w2.reshape(nc, FFN_CHUNK, d).astype(BF16)


def kernel(x, rel_bias, ln_g, ln_b, ffn_w13, ffn_w2, w_in, conv_w, a_log, dt_bias, dn_norm_g, sinks,
           w_branch_a, w_branch_b, w_out):
    b, l, d = x.shape
    depth = w_in.shape[0]
    alpha = (2 * depth) ** 0.25
    t = b * l
    tm = 512
    pos_bias = _pos_bias(rel_bias)
    x2 = x.reshape(t, d)
    row = lambda v: v.reshape(1, -1)
    colv = lambda v: v.reshape(-1, 1)
    for i in range(depth):
        x2 = _ffn_ln(x2, *_prep_ffn(ffn_w13[i, 0], ffn_w2[i, 0]), row(ln_g[i, 0]), row(ln_b[i, 0]),
                     alpha=alpha, tm=tm)
        w_main, w_small = _prep_w_in(w_in[i])
        hc, small = _in_proj(x2.reshape(b, l, d), w_main, w_small, tm=tm)
        ya = _attention(hc, pos_bias, sinks[i])
        yb = _delta(hc, small, conv_w[i], colv(a_log[i]), colv(dt_bias[i]), row(dn_norm_g[i]), n_chunks=2)
        x2 = _merge(x2, ya.reshape(t, Q_A), yb.reshape(t, V_B), hc.reshape(t, N_MAIN),
                    w_branch_a[i].astype(BF16), w_branch_b[i].astype(BF16), w_out[i].astype(BF16),
                    row(ln_g[i, 1]), row(ln_b[i, 1]), alpha=alpha, tm=tm)
        x2 = _ffn_ln(x2, *_prep_ffn(ffn_w13[i, 1], ffn_w2[i, 1]), row(ln_g[i, 2]), row(ln_b[i, 2]),
                     alpha=alpha, tm=tm)
    return x2.reshape(b, l, d)
```

```python
import functools
import math

import jax
import jax.numpy as jnp
from jax import lax
from jax.experimental import pallas as pl
from jax.experimental.pallas import tpu as pltpu

BF16 = jnp.bfloat16
F32 = jnp.float32

N_HEADS_A = 16
N_KV_A = 4
HEAD_DIM_A = 64
WINDOW = 128
N_HEADS_B = 8
KEY_DIM_B = 128
VAL_DIM_B = 128
CONV_K = 4
NUM_BUCKETS = 32
MAX_DISTANCE = 128
LN_EPS = 1e-5
NORM_EPS = 1e-6
NEG_INF = -1e30

LANES = 128
SUBLANES = 8
VMEM_LIMIT_BYTES = 58 * 1024 * 1024

Q_A = N_HEADS_A * HEAD_DIM_A
KDUP = N_KV_A * 2 * HEAD_DIM_A
QK_B = N_HEADS_B * KEY_DIM_B
V_B = N_HEADS_B * VAL_DIM_B
COL_QA = 0
COL_KA = COL_QA + Q_A
COL_VA = COL_KA + KDUP
COL_QB = COL_VA + KDUP
COL_KB = COL_QB + QK_B
COL_VB = COL_KB + QK_B
COL_Z = COL_VB + V_B
COL_GA = COL_Z + V_B
N_MAIN = COL_GA + 2 * Q_A

DELTA_CHUNK = 128
FFN_CHUNK = 256
PROJ_CHUNK = 512


def _cparams(sem):
    return pltpu.CompilerParams(dimension_semantics=sem, vmem_limit_bytes=VMEM_LIMIT_BYTES)


def _resident(shape):
    nd = len(shape)
    return pl.BlockSpec(shape, lambda *_: (0,) * nd, pipeline_mode=pl.Buffered(1))


def _layernorm(y, g, b):
    mu = jnp.mean(y, axis=-1, keepdims=True)
    d = y - mu
    var = jnp.mean(d * d, axis=-1, keepdims=True)
    return d * lax.rsqrt(var + LN_EPS) * g + b


def _silu(x):
    return x * jax.nn.sigmoid(x)


def _ffn_ln_kernel(x_ref, wg_ref, wu_ref, w2_ref, g_ref, b_ref, o_ref, *, alpha):
    x = x_ref[...]
    xb = x.astype(BF16)
    acc = jnp.zeros(x.shape, F32)
    for c in range(wg_ref.shape[0]):
        h = jnp.dot(xb, wg_ref[c], preferred_element_type=F32)
        u = jnp.dot(xb, wu_ref[c], preferred_element_type=F32)
        a = (_silu(h) * u).astype(BF16)
        acc = acc + jnp.dot(a, w2_ref[c], preferred_element_type=F32)
    o_ref[...] = _layernorm(alpha * x + 0.5 * acc, g_ref[...], b_ref[...])


def _ffn_ln(x2, wg, wu, w2, g, b, *, alpha, tm):
    t, d = x2.shape
    return pl.pallas_call(
        functools.partial(_ffn_ln_kernel, alpha=alpha),
        grid=(t // tm,),
        in_specs=[pl.BlockSpec((tm, d), lambda i: (i, 0)),
                  _resident(wg.shape), _resident(wu.shape), _resident(w2.shape),
                  _resident(g.shape), _resident(b.shape)],
        out_specs=pl.BlockSpec((tm, d), lambda i: (i, 0)),
        out_shape=jax.ShapeDtypeStruct((t, d), F32),
        compiler_params=_cparams(("parallel",)),
    )(x2, wg, wu, w2, g, b)


def _in_proj_kernel(x_ref, w_ref, ws_ref, cw_ref, hc_ref, kt_ref, sm_ref, raw_ref):
    tm = x_ref.shape[1]
    halo = SUBLANES
    per_chunk = PROJ_CHUNK // LANES
    n_conv = (COL_Z - COL_QB) // LANES

    @pl.when(pl.program_id(1) == 0)
    def _():
        raw_ref[:, 0:halo, :] = jnp.zeros((n_conv, halo, LANES), F32)

    xb = x_ref[0].astype(BF16)

    def proj(c):
        return jnp.dot(xb, w_ref[:, c * PROJ_CHUNK:(c + 1) * PROJ_CHUNK], preferred_element_type=F32)

    def conv_block(cb):
        acc = None
        for j in range(CONV_K):
            w = cw_ref[j:j + 1, cb * LANES:(cb + 1) * LANES]
            term = raw_ref[cb, halo - (CONV_K - 1) + j: halo - (CONV_K - 1) + j + tm, :] * w
            acc = term if acc is None else acc + term
        y = _silu(acc)
        kind = cb // N_HEADS_B
        if kind < 2:
            y = y * lax.rsqrt(jnp.sum(y * y, axis=-1, keepdims=True) + NORM_EPS)
        if kind == 0:
            y = y * (KEY_DIM_B ** -0.5)
        hc_ref[0, :, COL_QB + cb * LANES:COL_QB + (cb + 1) * LANES] = y.astype(BF16)
        if kind == 1:
            h = cb - N_HEADS_B
            kt_ref[0, h * LANES:(h + 1) * LANES, :] = y.T.astype(BF16)

    n_proj = N_MAIN // PROJ_CHUNK
    heavy = [c for c in range(n_proj) if COL_QB <= c * PROJ_CHUNK < COL_Z]
    light = [c for c in range(n_proj) if c not in heavy]
    order = []
    while heavy or light:
        if heavy:
            order.append(heavy.pop(0))
        if light:
            order.append(light.pop(0))
    for c in order:
        lo = c * PROJ_CHUNK
        r = proj(c)
        if lo < COL_QB:
            hc_ref[0, :, lo:lo + PROJ_CHUNK] = r.astype(BF16)
        elif lo < COL_Z:
            first = (lo - COL_QB) // LANES
            for i in range(per_chunk):
                raw_ref[first + i, halo:halo + tm, :] = r[:, i * LANES:(i + 1) * LANES]
            for i in range(per_chunk):
                conv_block(first + i)
        elif lo < COL_GA:
            hc_ref[0, :, lo:lo + PROJ_CHUNK] = _silu(r).astype(BF16)
        else:
            hc_ref[0, :, lo:lo + PROJ_CHUNK] = jax.nn.sigmoid(r).astype(BF16)
    small = jnp.dot(xb, ws_ref[...], preferred_element_type=F32)
    sm_ref[0] = small.T[:2 * N_HEADS_B, :]
    raw_ref[:, 0:halo, :] = raw_ref[:, tm:tm + halo, :]


def _in_proj(x3, w_main, w_small, conv_w, *, tm):
    b, l, d = x3.shape
    n_conv = (COL_Z - COL_QB) // LANES
    return pl.pallas_call(
        _in_proj_kernel,
        grid=(b, l // tm),
        in_specs=[pl.BlockSpec((1, tm, d), lambda i, j: (i, j, 0)),
                  _resident(w_main.shape), _resident(w_small.shape), _resident(conv_w.shape)],
        out_specs=[pl.BlockSpec((1, tm, N_MAIN), lambda i, j: (i, j, 0)),
                   pl.BlockSpec((1, QK_B, tm), lambda i, j: (i, 0, j)),
                   pl.BlockSpec((1, 2 * N_HEADS_B, tm), lambda i, j: (i, 0, j))],
        out_shape=[jax.ShapeDtypeStruct((b, l, N_MAIN), BF16),
                   jax.ShapeDtypeStruct((b, QK_B, l), BF16),
                   jax.ShapeDtypeStruct((b, 2 * N_HEADS_B, l), F32)],
        scratch_shapes=[pltpu.VMEM((n_conv, tm + SUBLANES, LANES), F32)],
        compiler_params=_cparams(("parallel", "arbitrary")),
    )(x3, w_main, w_small, conv_w)


def _t5_bucket(rel):
    n = jnp.maximum(rel, 0)
    max_exact = NUM_BUCKETS // 2
    nf = jnp.maximum(n, 1).astype(F32)
    large = max_exact + (jnp.log(nf / max_exact) / math.log(MAX_DISTANCE / max_exact)
                         * (NUM_BUCKETS - max_exact)).astype(jnp.int32)
    large = jnp.minimum(large, NUM_BUCKETS - 1)
    return jnp.where(n < max_exact, n, large)


def _pos_bias_kernel(rb_ref, bucket_ref, rel_ref, o_ref):
    h = pl.program_id(0)
    bucket = bucket_ref[...]
    rel = rel_ref[...]
    bias = jnp.zeros(bucket.shape, F32)
    for bk in range(NUM_BUCKETS):
        bias = jnp.where(bucket == bk, rb_ref[bk, h], bias)
    in_band = (rel >= 0) & (rel < WINDOW)
    own = lax.broadcasted_iota(jnp.int32, rel.shape, 1) >= WINDOW
    o_ref[0, 0] = jnp.where(in_band & own, bias, NEG_INF)
    o_ref[1, 0] = jnp.where(in_band, bias, NEG_INF)


def _pos_bias(rel_bias):
    r = jnp.arange(WINDOW, dtype=jnp.int32)[:, None]
    j = jnp.arange(2 * WINDOW, dtype=jnp.int32)[None, :]
    rel = r + WINDOW - j
    bucket = _t5_bucket(rel).astype(jnp.int32)
    full = pl.BlockSpec((WINDOW, 2 * WINDOW), lambda h: (0, 0))
    return pl.pallas_call(
        _pos_bias_kernel,
        grid=(N_HEADS_A,),
        in_specs=[pl.BlockSpec(memory_space=pltpu.SMEM), full, full],
        out_specs=pl.BlockSpec((2, 1, WINDOW, 2 * WINDOW), lambda h: (0, h, 0, 0)),
        out_shape=jax.ShapeDtypeStruct((2, N_HEADS_A, WINDOW, 2 * WINDOW), F32),
        compiler_params=_cparams(("arbitrary",)),
    )(rel_bias, bucket, rel)


def _attn_kernel(sink_ref, q_ref, kc_ref, kp_ref, vc_ref, vp_ref, bias_ref, o_ref):
    grp = N_HEADS_A // N_KV_A
    groups = range(N_KV_A)
    low = lax.broadcasted_iota(jnp.int32, (WINDOW, LANES), 1) < HEAD_DIM_A
    zero = jnp.zeros((WINDOW, LANES), BF16)
    ones = jnp.ones((2 * WINDOW, LANES), BF16)
    nt = (((1,), (1,)), ((), ()))

    def q_stack(g):
        parts = []
        for pair in range(grp // 2):
            qp = q_ref[0, :, pl.ds((g * grp + 2 * pair) * HEAD_DIM_A, LANES)]
            parts += [jnp.where(low, qp, zero), jnp.where(low, zero, qp)]
        return jnp.concatenate(parts, axis=0)

    def kv_cat(prev_ref, cur_ref, g):
        cols = slice(g * LANES, (g + 1) * LANES)
        return jnp.concatenate([prev_ref[0, :, cols], cur_ref[0, :, cols]], axis=0)

    sinks = [jnp.concatenate([jnp.full((WINDOW, LANES), sink_ref[g * grp + h], F32) for h in range(grp)], axis=0)
             for g in groups]
    q4s = [q_stack(g) for g in groups]
    kks = [kv_cat(kp_ref, kc_ref, g) for g in groups]
    vxs = [jnp.concatenate([kv_cat(vp_ref, vc_ref, g), ones], axis=1) for g in groups]
    ss = [lax.dot_general(q4, kk, nt, preferred_element_type=F32)
          + bias_ref[0, g * grp:(g + 1) * grp].reshape(grp * WINDOW, 2 * WINDOW)
          for g, q4, kk in zip(groups, q4s, kks)]
    ms = [jnp.maximum(jnp.broadcast_to(jnp.max(jnp.maximum(s[:, :LANES], s[:, LANES:]), axis=-1, keepdims=True),
                                       (grp * WINDOW, LANES)), sk) for s, sk in zip(ss, sinks)]
    ps = [jnp.exp(s - jnp.concatenate([m, m], axis=1)).astype(BF16) for s, m in zip(ss, ms)]
    ols = [jnp.dot(p, vx, preferred_element_type=F32) for p, vx in zip(ps, vxs)]
    for g, ol, m, sk in zip(groups, ols, ms, sinks):
        o = ol[:, :LANES] / (ol[:, LANES:] + jnp.exp(sk - m))
        for pair in range(grp // 2):
            r0 = (2 * pair) * WINDOW
            both = jnp.where(low, o[r0:r0 + WINDOW], o[r0 + WINDOW:r0 + 2 * WINDOW])
            o_ref[0, :, pl.ds((g * grp + 2 * pair) * HEAD_DIM_A, LANES)] = both.astype(BF16)


def _attention(hc, pos_bias, sink):
    b, l, _ = hc.shape
    nb = l // WINDOW
    kblk, vblk = COL_KA // KDUP, COL_VA // KDUP
    prev = lambda i, j: (i, jnp.maximum(j - 1, 0))
    return pl.pallas_call(
        _attn_kernel,
        grid=(b, nb),
        in_specs=[pl.BlockSpec(memory_space=pltpu.SMEM),
                  pl.BlockSpec((1, WINDOW, Q_A), lambda i, j: (i, j, COL_QA // Q_A)),
                  pl.BlockSpec((1, WINDOW, KDUP), lambda i, j: (i, j, kblk)),
                  pl.BlockSpec((1, WINDOW, KDUP), lambda i, j: (*prev(i, j), kblk)),
                  pl.BlockSpec((1, WINDOW, KDUP), lambda i, j: (i, j, vblk)),
                  pl.BlockSpec((1, WINDOW, KDUP), lambda i, j: (*prev(i, j), vblk)),
                  pl.BlockSpec((1, N_HEADS_A, WINDOW, 2 * WINDOW),
                               lambda i, j: (jnp.minimum(j, 1), 0, 0, 0))],
        out_specs=pl.BlockSpec((1, WINDOW, Q_A), lambda i, j: (i, j, 0)),
        out_shape=jax.ShapeDtypeStruct((b, l, Q_A), BF16),
        compiler_params=_cparams(("parallel", "arbitrary")),
    )(sink, hc, hc, hc, hc, hc, pos_bias)


def _chunk_scan(g, pos, reverse):
    n = g.shape[-1]
    s = 1
    while s < DELTA_CHUNK:
        if reverse:
            g = g + jnp.where(pos < DELTA_CHUNK - s, pltpu.roll(g, n - s, 1), 0.0)
        else:
            g = g + jnp.where(pos >= s, pltpu.roll(g, s, 1), 0.0)
        s *= 2
    return g


def _halves(t):
    return t[:, :LANES], t[:, LANES:]


def _pair(a, b):
    return jnp.concatenate([a, b], axis=1)


def _block_diag(t):
    a, b = _halves(t)
    z = jnp.zeros_like(a)
    return jnp.concatenate([_pair(a, z), _pair(z, b)], axis=0)


def _mm2(a, b):
    return jnp.dot(a.astype(BF16), _block_diag(b.astype(BF16)), preferred_element_type=F32)


def _unit_lower_inverses(lms, row, col):
    c = row.shape[0]
    eye = (row == col).astype(F32)
    base = SUBLANES
    same = lambda size: (row & -size) == (col & -size)
    lds = [jnp.where(same(base), lm, 0.0) for lm in lms]
    a1s = [eye - ld for ld in lds]
    ldbs = [ld.astype(BF16) for ld in lds]
    ld2bs = [_mm2(ldb, ldb).astype(BF16) for ldb in ldbs]
    ld4s = [_mm2(ld2b, ld2b) for ld2b in ld2bs]
    p1s = [a1 + _mm2(a1, ld2b) for a1, ld2b in zip(a1s, ld2bs)]
    xs = [p1 + _mm2(p1, ld4) for p1, ld4 in zip(p1s, ld4s)]
    lmbs = [lm.astype(BF16) for lm in lms]
    zero = jnp.zeros(row.shape, BF16)
    blk = base
    while blk < c:
        mask = same(2 * blk) & jnp.logical_not(same(blk))
        xbs = [x.astype(BF16) for x in xs]
        exs = [_mm2(jnp.where(mask, lmb, zero), xb) for lmb, xb in zip(lmbs, xbs)]
        xs = [x - _mm2(xb, ex) for x, xb, ex in zip(xs, xbs, exs)]
        blk *= 2
    return xs


def _delta_kernel(q_ref, k_ref, v_ref, kt_ref, z_ref, sm_ref, alog_ref, dtb_ref, ng_ref, o_ref, s_ref, *, n_chunks):
    c = DELTA_CHUNK
    blk = n_chunks * c

    @pl.when(pl.program_id(1) == 0)
    def _():
        s_ref[...] = jnp.zeros(s_ref.shape, F32)

    sm = sm_ref[0]
    pos = lax.broadcasted_iota(jnp.int32, (N_HEADS_B, blk), 1) & (c - 1)
    beta = jax.nn.sigmoid(sm[0:N_HEADS_B])
    glog = -jnp.exp(alog_ref[...]) * jax.nn.softplus(sm[N_HEADS_B:2 * N_HEADS_B] + dtb_ref[...])
    gc = _chunk_scan(glog, pos, reverse=False)
    tail = _chunk_scan(glog, pos, reverse=True) - glog
    etail = jnp.exp(tail)
    egc = jnp.exp(gc)
    rows = jnp.concatenate([beta, gc, egc, beta * egc, jnp.exp(gc + tail),
                            jnp.zeros((LANES - 5 * N_HEADS_B, blk), F32)], axis=0)
    cols = rows.T

    n_pairs = N_HEADS_B // 2
    wide = 2 * LANES
    row_i = lax.broadcasted_iota(jnp.int32, (c, wide), 0)
    col_i = lax.broadcasted_iota(jnp.int32, (c, wide), 1) & (LANES - 1)
    causal = row_i >= col_i
    strict = row_i > col_i

    def col2(kind, p, rows_):
        base = kind * N_HEADS_B + 2 * p
        return _pair(jnp.broadcast_to(cols[rows_, base:base + 1], (rows_.stop - rows_.start, LANES)),
                     jnp.broadcast_to(cols[rows_, base + 1:base + 2], (rows_.stop - rows_.start, LANES)))

    def row2(arr, p, t):
        return _pair(arr[2 * p:2 * p + 1, t], arr[2 * p + 1:2 * p + 2, t])

    items = [(p, ci) for ci in range(n_chunks) for p in range(n_pairs)]
    pt = [(p, slice(ci * c, (ci + 1) * c)) for p, ci in items]
    lanes_of = lambda p: slice(p * wide, (p + 1) * wide)
    qs = [q_ref[0, t, lanes_of(p)] for p, t in pt]
    ks = [k_ref[0, t, lanes_of(p)] for p, t in pt]
    vs = [v_ref[0, t, lanes_of(p)] for p, t in pt]
    kts = [_pair(kt_ref[0, 2 * p * LANES:(2 * p + 1) * LANES, t],
                 kt_ref[0, (2 * p + 1) * LANES:(2 * p + 2) * LANES, t]) for p, t in pt]
    betas = [col2(0, p, t) for p, t in pt]
    gccs = [col2(1, p, t) for p, t in pt]
    egcs = [col2(2, p, t) for p, t in pt]
    begs = [col2(3, p, t) for p, t in pt]
    gcrs = [row2(gc, p, t) for p, t in pt]
    etrs = [row2(etail, p, t) for p, t in pt]

    prods = [jnp.dot(jnp.concatenate([k, q], axis=0), _block_diag(kt), preferred_element_type=F32)
             for k, q, kt in zip(ks, qs, kts)]
    decays = [jnp.exp(jnp.where(causal, gcc - gcr, NEG_INF)) for gcc, gcr in zip(gccs, gcrs)]
    lms = [jnp.where(strict, beta * pr[:c] * d, 0.0) for beta, pr, d in zip(betas, prods, decays)]
    a_intras = [pr[c:] * d for pr, d in zip(prods, decays)]
    t_invs = _unit_lower_inverses(lms, row_i, col_i)
    uws = []
    for t_inv, k, v, beta, beg in zip(t_invs, ks, vs, betas, begs):
        (vb0, vb1), (kb0, kb1) = _halves((v * beta).astype(BF16)), _halves((k * beg).astype(BF16))
        z = jnp.zeros_like(vb0)
        rhs = jnp.concatenate([jnp.concatenate([vb0, kb0, z, z], axis=1),
                               jnp.concatenate([z, z, vb1, kb1], axis=1)], axis=0)
        uws.append(jnp.dot(t_inv.astype(BF16), rhs, preferred_element_type=F32))
    us = [_pair(uw[:, 0:LANES], uw[:, 2 * LANES:3 * LANES]) for uw in uws]
    wqs = [jnp.concatenate([_pair(uw[:, LANES:2 * LANES], uw[:, 3 * LANES:]).astype(BF16),
                            (q * egc).astype(BF16)], axis=0)
           for uw, q, egc in zip(uws, qs, egcs)]
    akts = [jnp.concatenate([a.astype(BF16), (kt * etr).astype(BF16)], axis=0)
            for a, kt, etr in zip(a_intras, kts, etrs)]

    states = [s_ref[p] for p in range(n_pairs)]
    for ci in range(n_chunks):
        idx = [ci * n_pairs + p for p in range(n_pairs)]
        tok = slice(ci * c, (ci + 1) * c)
        r1 = [jnp.dot(wqs[i], _block_diag(states[p].astype(BF16)), preferred_element_type=F32)
              for p, i in enumerate(idx)]
        v_news = [us[i] - r[:c] for i, r in zip(idx, r1)]
        r2 = [jnp.dot(akts[i], _block_diag(vn.astype(BF16)), preferred_element_type=F32)
              for i, vn in zip(idx, v_news)]
        for p in range(n_pairs):
            base = 4 * N_HEADS_B + 2 * p
            egl = _pair(jnp.broadcast_to(cols[ci * c:ci * c + 1, base:base + 1], (1, LANES)),
                        jnp.broadcast_to(cols[ci * c:ci * c + 1, base + 1:base + 2], (1, LANES)))
            states[p] = states[p] * egl + r2[p][c:]
            o = r1[p][c:] + r2[p][:c]
            o = _pair(*[u * lax.rsqrt(jnp.mean(u * u, axis=-1, keepdims=True) + NORM_EPS) * ng_ref[...]
                        for u in _halves(o)])
            o_ref[0, tok, lanes_of(p)] = (o * z_ref[0, tok, lanes_of(p)]).astype(BF16)
    for p in range(n_pairs):
        s_ref[p] = states[p]


def _delta(hc, kt, small, a_log, dt_bias, norm_g, *, n_chunks):
    b, l, _ = hc.shape
    blk = n_chunks * DELTA_CHUNK
    col = lambda cidx: pl.BlockSpec((1, blk, QK_B), lambda i, j: (i, j, cidx))
    return pl.pallas_call(
        functools.partial(_delta_kernel, n_chunks=n_chunks),
        grid=(b, l // blk),
        in_specs=[col(COL_QB // QK_B), col(COL_KB // QK_B), col(COL_VB // QK_B),
                  pl.BlockSpec((1, QK_B, blk), lambda i, j: (i, 0, j)),
                  col(COL_Z // QK_B),
                  pl.BlockSpec((1, 2 * N_HEADS_B, blk), lambda i, j: (i, 0, j)),
                  _resident(a_log.shape), _resident(dt_bias.shape), _resident(norm_g.shape)],
        out_specs=pl.BlockSpec((1, blk, V_B), lambda i, j: (i, j, 0)),
        out_shape=jax.ShapeDtypeStruct((b, l, V_B), BF16),
        scratch_shapes=[pltpu.VMEM((N_HEADS_B // 2, KEY_DIM_B, 2 * VAL_DIM_B), F32)],
        compiler_params=_cparams(("parallel", "arbitrary")),
    )(hc, hc, hc, kt, hc, small, a_log, dt_bias, norm_g)


def _merge_kernel(x_ref, ya_ref, yb_ref, ga_ref, gb_ref, wa_ref, wb_ref, wo_ref, g_ref, b_ref, o_ref, *, alpha):
    ya = jnp.dot(ya_ref[...], wa_ref[...], preferred_element_type=F32)
    yb = jnp.dot(yb_ref[...], wb_ref[...], preferred_element_type=F32)
    mix = ga_ref[...] * ya + gb_ref[...] * yb
    y = jnp.dot(mix.astype(BF16), wo_ref[...], preferred_element_type=F32)
    o_ref[...] = _layernorm(alpha * x_ref[...] + y, g_ref[...], b_ref[...])


def _merge(x2, ya, yb, hc2, wa, wb, wo, g, b, *, alpha, tm):
    t, d = x2.shape
    tile = lambda cidx=0: pl.BlockSpec((tm, d), lambda i: (i, cidx))
    return pl.pallas_call(
        functools.partial(_merge_kernel, alpha=alpha),
        grid=(t // tm,),
        in_specs=[tile(), tile(), tile(), tile(COL_GA // Q_A), tile(COL_GA // Q_A + 1),
                  _resident(wa.shape), _resident(wb.shape), _resident(wo.shape),
                  _resident(g.shape), _resident(b.shape)],
        out_specs=tile(),
        out_shape=jax.ShapeDtypeStruct((t, d), F32),
        compiler_params=_cparams(("parallel",)),
    )(x2, ya, yb, hc2, hc2, wa, wb, wo, g, b)


def _prep_w_in(w_in):
    d = w_in.shape[0]
    o = 0
    parts = {}
    for name, size in (("qa", Q_A), ("ka", N_KV_A * HEAD_DIM_A), ("va", N_KV_A * HEAD_DIM_A),
                       ("qkv", 2 * QK_B + V_B), ("beta", N_HEADS_B), ("dt", N_HEADS_B),
                       ("z", V_B), ("gate", 2 * Q_A)):
        parts[name] = w_in[:, o:o + size]
        o += size

    def dup(w):
        w = w.reshape(d, N_KV_A, 1, HEAD_DIM_A)
        return jnp.broadcast_to(w, (d, N_KV_A, 2, HEAD_DIM_A)).reshape(d, KDUP)

    w_main = jnp.concatenate([parts["qa"] * (HEAD_DIM_A ** -0.5), dup(parts["ka"]), dup(parts["va"]), parts["qkv"],
                              parts["z"], parts["gate"]], axis=1).astype(BF16)
    w_small = jnp.concatenate([parts["beta"], parts["dt"],
                               jnp.zeros((d, LANES - 2 * N_HEADS_B), w_in.dtype)], axis=1).astype(BF16)
    return w_main, w_small


def _prep_ffn(w13, w2):
    d, f2 = w13.shape
    f = f2 // 2
    nc = f // FFN_CHUNK
    wg = w13[:, :f].reshape(d, nc, FFN_CHUNK).transpose(1, 0, 2).astype(BF16)
    wu = w13[:, f:].reshape(d, nc, FFN_CHUNK).transpose(1, 0, 2).astype(BF16)
    return wg, wu, w2.reshape(nc, FFN_CHUNK, d).astype(BF16)


def kernel(x, rel_bias, ln_g, ln_b, ffn_w13, ffn_w2, w_in, conv_w, a_log, dt_bias, dn_norm_g, sinks,
           w_branch_a, w_branch_b, w_out):
    b, l, d = x.shape
    depth = w_in.shape[0]
    alpha = (2 * depth) ** 0.25
    t = b * l
    tm = 512
    pos_bias = _pos_bias(rel_bias)
    x2 = x.reshape(t, d)
    row = lambda v: v.reshape(1, -1)
    colv = lambda v: v.reshape(-1, 1)
    for i in range(depth):
        x2 = _ffn_ln(x2, *_prep_ffn(ffn_w13[i, 0], ffn_w2[i, 0]), row(ln_g[i, 0]), row(ln_b[i, 0]),
                     alpha=alpha, tm=tm)
        w_main, w_small = _prep_w_in(w_in[i])
        hc, kt, small = _in_proj(x2.reshape(b, l, d), w_main, w_small, conv_w[i], tm=tm)
        ya = _attention(hc, pos_bias, sinks[i])
        yb = _delta(hc, kt, small, colv(a_log[i]), colv(dt_bias[i]), row(dn_norm_g[i]), n_chunks=4)
        x2 = _merge(x2, ya.reshape(t, Q_A), yb.reshape(t, V_B), hc.reshape(t, N_MAIN),
                    w_branch_a[i].astype(BF16), w_branch_b[i].astype(BF16), w_out[i].astype(BF16),
                    row(ln_g[i, 1]), row(ln_b[i, 1]), alpha=alpha, tm=tm)
        x2 = _ffn_ln(x2, *_prep_ffn(ffn_w13[i, 1], ffn_w2[i, 1]), row(ln_g[i, 2]), row(ln_b[i, 2]),
                     alpha=alpha, tm=tm)
    return x2.reshape(b, l, d)
```

```python
import functools
import math

import jax
import jax.numpy as jnp
from jax import lax
from jax.experimental import pallas as pl
from jax.experimental.pallas import tpu as pltpu

BF16 = jnp.bfloat16
F32 = jnp.float32

N_HEADS_A = 16
N_KV_A = 4
HEAD_DIM_A = 64
WINDOW = 128
N_HEADS_B = 8
KEY_DIM_B = 128
VAL_DIM_B = 128
CONV_K = 4
NUM_BUCKETS = 32
MAX_DISTANCE = 128
LN_EPS = 1e-5
NORM_EPS = 1e-6
NEG_INF = -1e30

LANES = 128
SUBLANES = 8
VMEM_LIMIT_BYTES = 58 * 1024 * 1024

Q_A = N_HEADS_A * HEAD_DIM_A
KDUP = N_KV_A * 2 * HEAD_DIM_A
QK_B = N_HEADS_B * KEY_DIM_B
V_B = N_HEADS_B * VAL_DIM_B
COL_QA = 0
COL_KA = COL_QA + Q_A
COL_VA = COL_KA + KDUP
COL_QB = COL_VA + KDUP
COL_KB = COL_QB + QK_B
COL_VB = COL_KB + QK_B
COL_Z = COL_VB + V_B
COL_GA = COL_Z + V_B
N_MAIN = COL_GA + 2 * Q_A

KV_A = N_KV_A * HEAD_DIM_A
W_KV = Q_A
W_QB = W_KV + 2 * KV_A
W_Z = W_QB + 2 * QK_B + V_B
W_GA = W_Z + V_B
W_MAIN = W_GA + 2 * Q_A

DELTA_CHUNK = 128
FFN_CHUNK = 256
PROJ_CHUNK = 512
ATTN_BLOCKS = 2


def _cparams(sem):
    return pltpu.CompilerParams(dimension_semantics=sem, vmem_limit_bytes=VMEM_LIMIT_BYTES)


def _resident(shape):
    nd = len(shape)
    return pl.BlockSpec(shape, lambda *_: (0,) * nd, pipeline_mode=pl.Buffered(1))


def _layernorm(y, g, b):
    mu = jnp.mean(y, axis=-1, keepdims=True)
    d = y - mu
    var = jnp.mean(d * d, axis=-1, keepdims=True)
    return d * lax.rsqrt(var + LN_EPS) * g + b


def _silu(x):
    return x * jax.nn.sigmoid(x)


def _ffn_ln_kernel(x_ref, wg_ref, wu_ref, w2_ref, g_ref, b_ref, o_ref, *, alpha):
    half = x_ref.shape[0] // 2
    for part in range(2):
        rows = slice(part * half, (part + 1) * half)
        x = x_ref[rows, :]
        xb = x.astype(BF16)
        acc = jnp.zeros(x.shape, F32)
        for c in range(wg_ref.shape[1] // FFN_CHUNK):
            cols = slice(c * FFN_CHUNK, (c + 1) * FFN_CHUNK)
            h = jnp.dot(xb, wg_ref[:, cols], preferred_element_type=F32)
            u = jnp.dot(xb, wu_ref[:, cols], preferred_element_type=F32)
            a = (_silu(h) * u).astype(BF16)
            acc = acc + jnp.dot(a, w2_ref[cols, :], preferred_element_type=F32)
        o_ref[rows, :] = _layernorm(alpha * x + 0.5 * acc, g_ref[...], b_ref[...])


def _ffn_ln(x2, wg, wu, w2, g, b, *, alpha, tm):
    t, d = x2.shape
    return pl.pallas_call(
        functools.partial(_ffn_ln_kernel, alpha=alpha),
        grid=(t // tm,),
        in_specs=[pl.BlockSpec((tm, d), lambda i: (i, 0)),
                  _resident(wg.shape), _resident(wu.shape), _resident(w2.shape),
                  _resident(g.shape), _resident(b.shape)],
        out_specs=pl.BlockSpec((tm, d), lambda i: (i, 0)),
        out_shape=jax.ShapeDtypeStruct((t, d), F32),
        compiler_params=_cparams(("parallel",)),
    )(x2, wg, wu, w2, g, b)


def _in_proj_kernel(x_ref, w_ref, ws_ref, cw_ref, hc_ref, kt_ref, sm_ref, raw_ref):
    tm = x_ref.shape[1]
    halo = SUBLANES
    per_chunk = PROJ_CHUNK // LANES
    n_conv = (COL_Z - COL_QB) // LANES

    @pl.when(pl.program_id(1) == 0)
    def _():
        raw_ref[:, 0:halo, :] = jnp.zeros((n_conv, halo, LANES), F32)

    xb = x_ref[0].astype(BF16)

    def proj(c):
        return jnp.dot(xb, w_ref[:, c * PROJ_CHUNK:(c + 1) * PROJ_CHUNK], preferred_element_type=F32)

    def conv_block(cb):
        acc = None
        for j in range(CONV_K):
            w = cw_ref[j:j + 1, cb * LANES:(cb + 1) * LANES]
            term = raw_ref[cb, halo - (CONV_K - 1) + j: halo - (CONV_K - 1) + j + tm, :] * w
            acc = term if acc is None else acc + term
        y = _silu(acc)
        kind = cb // N_HEADS_B
        if kind < 2:
            y = y * lax.rsqrt(jnp.sum(y * y, axis=-1, keepdims=True) + NORM_EPS)
        if kind == 0:
            y = y * (KEY_DIM_B ** -0.5)
        hc_ref[0, :, COL_QB + cb * LANES:COL_QB + (cb + 1) * LANES] = y.astype(BF16)
        if kind == 1:
            h = cb - N_HEADS_B
            kt_ref[0, h * LANES:(h + 1) * LANES, :] = y.T.astype(BF16)

    low = lax.broadcasted_iota(jnp.int32, (tm, LANES), 1) < HEAD_DIM_A

    def store_twice(r, col):
        for i in range(KV_A // LANES):
            two = r[:, i * LANES:(i + 1) * LANES]
            swapped = pltpu.roll(two, HEAD_DIM_A, 1)
            hc_ref[0, :, col + 2 * i * LANES:col + (2 * i + 1) * LANES] = jnp.where(low, two, swapped).astype(BF16)
            hc_ref[0, :, col + (2 * i + 1) * LANES:col + (2 * i + 2) * LANES] = jnp.where(low, swapped, two).astype(BF16)

    n_proj = W_MAIN // PROJ_CHUNK
    heavy = [c for c in range(n_proj) if W_QB <= c * PROJ_CHUNK < W_Z]
    light = [c for c in range(n_proj) if c not in heavy]
    order = []
    while heavy or light:
        if heavy:
            order.append(heavy.pop(0))
        if light:
            order.append(light.pop(0))
    for c in order:
        lo = c * PROJ_CHUNK
        r = proj(c)
        if lo < W_KV:
            hc_ref[0, :, COL_QA + lo:COL_QA + lo + PROJ_CHUNK] = r.astype(BF16)
        elif lo < W_QB:
            store_twice(r[:, :KV_A], COL_KA)
            store_twice(r[:, KV_A:], COL_VA)
        elif lo < W_Z:
            first = (lo - W_QB) // LANES
            for i in range(per_chunk):
                raw_ref[first + i, halo:halo + tm, :] = r[:, i * LANES:(i + 1) * LANES]
            for i in range(per_chunk):
                conv_block(first + i)
        elif lo < W_GA:
            hc_ref[0, :, COL_Z + lo - W_Z:COL_Z + lo - W_Z + PROJ_CHUNK] = _silu(r).astype(BF16)
        else:
            hc_ref[0, :, COL_GA + lo - W_GA:COL_GA + lo - W_GA + PROJ_CHUNK] = jax.nn.sigmoid(r).astype(BF16)
    small = jnp.dot(xb, ws_ref[...], preferred_element_type=F32)
    sm_ref[0] = small.T[:2 * N_HEADS_B, :]
    raw_ref[:, 0:halo, :] = raw_ref[:, tm:tm + halo, :]


def _in_proj(x3, w_main, w_small, conv_w, *, tm):
    b, l, d = x3.shape
    n_conv = (COL_Z - COL_QB) // LANES
    return pl.pallas_call(
        _in_proj_kernel,
        grid=(b, l // tm),
        in_specs=[pl.BlockSpec((1, tm, d), lambda i, j: (i, j, 0)),
                  _resident(w_main.shape), _resident(w_small.shape), _resident(conv_w.shape)],
        out_specs=[pl.BlockSpec((1, tm, N_MAIN), lambda i, j: (i, j, 0)),
                   pl.BlockSpec((1, QK_B, tm), lambda i, j: (i, 0, j)),
                   pl.BlockSpec((1, 2 * N_HEADS_B, tm), lambda i, j: (i, 0, j))],
        out_shape=[jax.ShapeDtypeStruct((b, l, N_MAIN), BF16),
                   jax.ShapeDtypeStruct((b, QK_B, l), BF16),
                   jax.ShapeDtypeStruct((b, 2 * N_HEADS_B, l), F32)],
        scratch_shapes=[pltpu.VMEM((n_conv, tm + SUBLANES, LANES), F32)],
        compiler_params=_cparams(("parallel", "arbitrary")),
    )(x3, w_main, w_small, conv_w)


def _t5_bucket(rel):
    n = jnp.maximum(rel, 0)
    max_exact = NUM_BUCKETS // 2
    nf = jnp.maximum(n, 1).astype(F32)
    large = max_exact + (jnp.log(nf / max_exact) / math.log(MAX_DISTANCE / max_exact)
                         * (NUM_BUCKETS - max_exact)).astype(jnp.int32)
    large = jnp.minimum(large, NUM_BUCKETS - 1)
    return jnp.where(n < max_exact, n, large)


def _pos_bias_kernel(rb_ref, bucket_ref, rel_ref, o_ref):
    h = pl.program_id(0)
    bucket = bucket_ref[...]
    rel = rel_ref[...]
    bias = jnp.zeros(bucket.shape, F32)
    for bk in range(NUM_BUCKETS):
        bias = jnp.where(bucket == bk, rb_ref[bk, h], bias)
    in_band = (rel >= 0) & (rel < WINDOW)
    own = lax.broadcasted_iota(jnp.int32, rel.shape, 1) >= WINDOW
    o_ref[0, 0] = jnp.where(in_band & own, bias, NEG_INF)
    o_ref[1, 0] = jnp.where(in_band, bias, NEG_INF)


def _pos_bias(rel_bias):
    r = jnp.arange(WINDOW, dtype=jnp.int32)[:, None]
    j = jnp.arange(2 * WINDOW, dtype=jnp.int32)[None, :]
    rel = r + WINDOW - j
    bucket = _t5_bucket(rel).astype(jnp.int32)
    full = pl.BlockSpec((WINDOW, 2 * WINDOW), lambda h: (0, 0))
    return pl.pallas_call(
        _pos_bias_kernel,
        grid=(N_HEADS_A,),
        in_specs=[pl.BlockSpec(memory_space=pltpu.SMEM), full, full],
        out_specs=pl.BlockSpec((2, 1, WINDOW, 2 * WINDOW), lambda h: (0, h, 0, 0)),
        out_shape=jax.ShapeDtypeStruct((2, N_HEADS_A, WINDOW, 2 * WINDOW), F32),
        compiler_params=_cparams(("arbitrary",)),
    )(rel_bias, bucket, rel)


def _attn_kernel(sink_ref, q_ref, kc_ref, kp_ref, vc_ref, vp_ref, bias0_ref, bias1_ref, o_ref):
    grp = N_HEADS_A // N_KV_A
    items = [(qb, g) for qb in range(ATTN_BLOCKS) for g in range(N_KV_A)]
    low = lax.broadcasted_iota(jnp.int32, (WINDOW, LANES), 1) < HEAD_DIM_A
    zero = jnp.zeros((WINDOW, LANES), BF16)
    ones = jnp.ones((2 * WINDOW, LANES), BF16)
    nt = (((1,), (1,)), ((), ()))

    def q_stack(qb, g):
        rows = slice(qb * WINDOW, (qb + 1) * WINDOW)
        parts = []
        for pair in range(grp // 2):
            qp = q_ref[0, rows, pl.ds((g * grp + 2 * pair) * HEAD_DIM_A, LANES)]
            parts += [jnp.where(low, qp, zero), jnp.where(low, zero, qp)]
        return jnp.concatenate(parts, axis=0)

    def kv_cat(prev_ref, cur_ref, qb, g):
        cols = slice(g * LANES, (g + 1) * LANES)
        if qb == 0:
            return jnp.concatenate([prev_ref[0, :, cols], cur_ref[0, 0:WINDOW, cols]], axis=0)
        return cur_ref[0, (qb - 1) * WINDOW:(qb + 1) * WINDOW, cols]

    def bias(qb, g):
        ref = bias0_ref if qb == 0 else bias1_ref
        return ref[0, g * grp:(g + 1) * grp].reshape(grp * WINDOW, 2 * WINDOW)

    sinks = [jnp.concatenate([jnp.full((WINDOW, LANES), sink_ref[g * grp + h], F32) for h in range(grp)], axis=0)
             for g in range(N_KV_A)]
    q4s = [q_stack(qb, g) for qb, g in items]
    kks = [kv_cat(kp_ref, kc_ref, qb, g) for qb, g in items]
    vxs = [jnp.concatenate([kv_cat(vp_ref, vc_ref, qb, g), ones], axis=1) for qb, g in items]
    ss = [lax.dot_general(q4, kk, nt, preferred_element_type=F32) + bias(qb, g)
          for (qb, g), q4, kk in zip(items, q4s, kks)]
    ms = [jnp.maximum(jnp.broadcast_to(jnp.max(jnp.maximum(s[:, :LANES], s[:, LANES:]), axis=-1, keepdims=True),
                                       (grp * WINDOW, LANES)), sinks[g]) for (_, g), s in zip(items, ss)]
    ps = [jnp.exp(s - jnp.concatenate([m, m], axis=1)).astype(BF16) for s, m in zip(ss, ms)]
    ols = [jnp.dot(p, vx, preferred_element_type=F32) for p, vx in zip(ps, vxs)]
    for (qb, g), ol, m in zip(items, ols, ms):
        o = ol[:, :LANES] / (ol[:, LANES:] + jnp.exp(sinks[g] - m))
        for pair in range(grp // 2):
            r0 = (2 * pair) * WINDOW
            both = jnp.where(low, o[r0:r0 + WINDOW], o[r0 + WINDOW:r0 + 2 * WINDOW])
            o_ref[0, qb * WINDOW:(qb + 1) * WINDOW, pl.ds((g * grp + 2 * pair) * HEAD_DIM_A, LANES)] = both.astype(BF16)


def _attention(hc, pos_bias, sink):
    b, l, _ = hc.shape
    rows = ATTN_BLOCKS * WINDOW
    kblk, vblk = COL_KA // KDUP, COL_VA // KDUP
    prev = lambda i, j: (i, jnp.maximum(ATTN_BLOCKS * j - 1, 0))
    bias_spec = lambda f: pl.BlockSpec((1, N_HEADS_A, WINDOW, 2 * WINDOW), f)
    return pl.pallas_call(
        _attn_kernel,
        grid=(b, l // rows),
        in_specs=[pl.BlockSpec(memory_space=pltpu.SMEM),
                  pl.BlockSpec((1, rows, Q_A), lambda i, j: (i, j, COL_QA // Q_A)),
                  pl.BlockSpec((1, rows, KDUP), lambda i, j: (i, j, kblk)),
                  pl.BlockSpec((1, WINDOW, KDUP), lambda i, j: (*prev(i, j), kblk)),
                  pl.BlockSpec((1, rows, KDUP), lambda i, j: (i, j, vblk)),
                  pl.BlockSpec((1, WINDOW, KDUP), lambda i, j: (*prev(i, j), vblk)),
                  bias_spec(lambda i, j: (jnp.minimum(j, 1), 0, 0, 0)),
                  bias_spec(lambda i, j: (1, 0, 0, 0))],
        out_specs=pl.BlockSpec((1, rows, Q_A), lambda i, j: (i, j, 0)),
        out_shape=jax.ShapeDtypeStruct((b, l, Q_A), BF16),
        compiler_params=_cparams(("parallel", "arbitrary")),
    )(sink, hc, hc, hc, hc, hc, pos_bias, pos_bias)


def _chunk_sums(g):
    c = DELTA_CHUNK
    n = g.shape[1] // c
    hi = g.astype(BF16).astype(F32)
    r1 = g - hi
    mid = r1.astype(BF16).astype(F32)
    lo = r1 - mid
    s = lax.broadcasted_iota(jnp.int32, (c, 2 * c), 0)
    t = lax.broadcasted_iota(jnp.int32, (c, 2 * c), 1)
    tri = jnp.where(((t < c) & (s <= t)) | ((t >= c) & (s > t - c)), 1.0, 0.0).astype(BF16)
    lhs = jnp.concatenate([piece[:, ci * c:(ci + 1) * c] for ci in range(n) for piece in (hi, mid, lo)],
                          axis=0).astype(BF16)
    out = jnp.dot(lhs, tri, preferred_element_type=F32)
    k = g.shape[0]
    both = [out[3 * k * ci:3 * k * ci + k] + out[3 * k * ci + k:3 * k * ci + 2 * k]
            + out[3 * k * ci + 2 * k:3 * k * (ci + 1)] for ci in range(n)]
    prefix = jnp.concatenate([bt[:, :c] for bt in both], axis=1)
    suffix = jnp.concatenate([bt[:, c:] for bt in both], axis=1)
    return prefix, suffix


def _halves(t):
    return t[:, :LANES], t[:, LANES:]


def _pair(a, b):
    return jnp.concatenate([a, b], axis=1)


def _block_diag(t):
    a, b = _halves(t)
    z = jnp.zeros_like(a)
    return jnp.concatenate([_pair(a, z), _pair(z, b)], axis=0)


def _mm2(a, b):
    return jnp.dot(a.astype(BF16), _block_diag(b.astype(BF16)), preferred_element_type=F32)


def _unit_lower_inverses(lms, row, col):
    c = row.shape[0]
    eye = (row == col).astype(F32)
    base = SUBLANES
    same = lambda size: (row & -size) == (col & -size)
    lds = [jnp.where(same(base), lm, 0.0) for lm in lms]
    a1s = [eye - ld for ld in lds]
    ldbs = [ld.astype(BF16) for ld in lds]
    ld2bs = [_mm2(ldb, ldb).astype(BF16) for ldb in ldbs]
    ld4s = [_mm2(ld2b, ld2b) for ld2b in ld2bs]
    p1s = [a1 + _mm2(a1, ld2b) for a1, ld2b in zip(a1s, ld2bs)]
    xs = [p1 + _mm2(p1, ld4) for p1, ld4 in zip(p1s, ld4s)]
    lmbs = [lm.astype(BF16) for lm in lms]
    zero = jnp.zeros(row.shape, BF16)
    blk = base
    while blk < c:
        mask = same(2 * blk) & jnp.logical_not(same(blk))
        xbs = [x.astype(BF16) for x in xs]
        exs = [_mm2(jnp.where(mask, lmb, zero), xb) for lmb, xb in zip(lmbs, xbs)]
        xs = [x - _mm2(xb, ex) for x, xb, ex in zip(xs, xbs, exs)]
        blk *= 2
    return xs


def _delta_kernel(q_ref, k_ref, v_ref, kt_ref, z_ref, sm_ref, alog_ref, dtb_ref, ng_ref, o_ref, s_ref, *, n_chunks):
    c = DELTA_CHUNK
    blk = n_chunks * c

    @pl.when(pl.program_id(1) == 0)
    def _():
        s_ref[...] = jnp.zeros(s_ref.shape, F32)

    sm = sm_ref[0]
    beta = jax.nn.sigmoid(sm[0:N_HEADS_B])
    glog = -jnp.exp(alog_ref[...]) * jax.nn.softplus(sm[N_HEADS_B:2 * N_HEADS_B] + dtb_ref[...])
    gc, tail = _chunk_sums(glog)
    etail = jnp.exp(tail)
    egc = jnp.exp(gc)
    rows = jnp.concatenate([beta, gc, egc, beta * egc, jnp.exp(gc + tail),
                            jnp.zeros((LANES - 5 * N_HEADS_B, blk), F32)], axis=0)
    cols = rows.T

    n_pairs = N_HEADS_B // 2
    wide = 2 * LANES
    row_i = lax.broadcasted_iota(jnp.int32, (c, wide), 0)
    col_i = lax.broadcasted_iota(jnp.int32, (c, wide), 1) & (LANES - 1)
    causal = row_i >= col_i
    strict = row_i > col_i

    def col2(kind, p, rows_):
        base = kind * N_HEADS_B + 2 * p
        return _pair(jnp.broadcast_to(cols[rows_, base:base + 1], (rows_.stop - rows_.start, LANES)),
                     jnp.broadcast_to(cols[rows_, base + 1:base + 2], (rows_.stop - rows_.start, LANES)))

    def row2(arr, p, t):
        return _pair(arr[2 * p:2 * p + 1, t], arr[2 * p + 1:2 * p + 2, t])

    items = [(p, ci) for ci in range(n_chunks) for p in range(n_pairs)]
    pt = [(p, slice(ci * c, (ci + 1) * c)) for p, ci in items]
    lanes_of = lambda p: slice(p * wide, (p + 1) * wide)
    qs = [q_ref[0, t, lanes_of(p)] for p, t in pt]
    ks = [k_ref[0, t, lanes_of(p)] for p, t in pt]
    vs = [v_ref[0, t, lanes_of(p)] for p, t in pt]
    kts = [_pair(kt_ref[0, 2 * p * LANES:(2 * p + 1) * LANES, t],
                 kt_ref[0, (2 * p + 1) * LANES:(2 * p + 2) * LANES, t]) for p, t in pt]
    betas = [col2(0, p, t) for p, t in pt]
    gccs = [col2(1, p, t) for p, t in pt]
    egcs = [col2(2, p, t) for p, t in pt]
    begs = [col2(3, p, t) for p, t in pt]
    gcrs = [row2(gc, p, t) for p, t in pt]
    etrs = [row2(etail, p, t) for p, t in pt]

    prods = [jnp.dot(jnp.concatenate([k, q], axis=0), _block_diag(kt), preferred_element_type=F32)
             for k, q, kt in zip(ks, qs, kts)]
    decays = [jnp.exp(jnp.where(causal, gcc - gcr, NEG_INF)) for gcc, gcr in zip(gccs, gcrs)]
    lms = [jnp.where(strict, beta * pr[:c] * d, 0.0) for beta, pr, d in zip(betas, prods, decays)]
    a_intras = [pr[c:] * d for pr, d in zip(prods, decays)]
    t_invs = _unit_lower_inverses(lms, row_i, col_i)
    uws = []
    for t_inv, k, v, beta, beg in zip(t_invs, ks, vs, betas, begs):
        (vb0, vb1), (kb0, kb1) = _halves((v * beta).astype(BF16)), _halves((k * beg).astype(BF16))
        z = jnp.zeros_like(vb0)
        rhs = jnp.concatenate([jnp.concatenate([vb0, kb0, z, z], axis=1),
                               jnp.concatenate([z, z, vb1, kb1], axis=1)], axis=0)
        uws.append(jnp.dot(t_inv.astype(BF16), rhs, preferred_element_type=F32))
    us = [_pair(uw[:, 0:LANES], uw[:, 2 * LANES:3 * LANES]) for uw in uws]
    wqs = [jnp.concatenate([_pair(uw[:, LANES:2 * LANES], uw[:, 3 * LANES:]).astype(BF16),
                            (q * egc).astype(BF16)], axis=0)
           for uw, q, egc in zip(uws, qs, egcs)]
    akts = [jnp.concatenate([a.astype(BF16), (kt * etr).astype(BF16)], axis=0)
            for a, kt, etr in zip(a_intras, kts, etrs)]

    states = [s_ref[p] for p in range(n_pairs)]
    for ci in range(n_chunks):
        idx = [ci * n_pairs + p for p in range(n_pairs)]
        tok = slice(ci * c, (ci + 1) * c)
        r1 = [jnp.dot(wqs[i], _block_diag(states[p].astype(BF16)), preferred_element_type=F32)
              for p, i in enumerate(idx)]
        v_news = [us[i] - r[:c] for i, r in zip(idx, r1)]
        r2 = [jnp.dot(akts[i], _block_diag(vn.astype(BF16)), preferred_element_type=F32)
              for i, vn in zip(idx, v_news)]
        for p in range(n_pairs):
            base = 4 * N_HEADS_B + 2 * p
            egl = _pair(jnp.broadcast_to(cols[ci * c:ci * c + 1, base:base + 1], (1, LANES)),
                        jnp.broadcast_to(cols[ci * c:ci * c + 1, base + 1:base + 2], (1, LANES)))
            states[p] = states[p] * egl + r2[p][c:]
            o = r1[p][c:] + r2[p][:c]
            o = _pair(*[u * lax.rsqrt(jnp.mean(u * u, axis=-1, keepdims=True) + NORM_EPS) * ng_ref[...]
                        for u in _halves(o)])
            o_ref[0, tok, lanes_of(p)] = (o * z_ref[0, tok, lanes_of(p)]).astype(BF16)
    for p in range(n_pairs):
        s_ref[p] = states[p]


def _delta(hc, kt, small, a_log, dt_bias, norm_g, *, n_chunks):
    b, l, _ = hc.shape
    blk = n_chunks * DELTA_CHUNK
    col = lambda cidx: pl.BlockSpec((1, blk, QK_B), lambda i, j: (i, j, cidx))
    return pl.pallas_call(
        functools.partial(_delta_kernel, n_chunks=n_chunks),
        grid=(b, l // blk),
        in_specs=[col(COL_QB // QK_B), col(COL_KB // QK_B), col(COL_VB // QK_B),
                  pl.BlockSpec((1, QK_B, blk), lambda i, j: (i, 0, j)),
                  col(COL_Z // QK_B),
                  pl.BlockSpec((1, 2 * N_HEADS_B, blk), lambda i, j: (i, 0, j)),
                  _resident(a_log.shape), _resident(dt_bias.shape), _resident(norm_g.shape)],
        out_specs=pl.BlockSpec((1, blk, V_B), lambda i, j: (i, j, 0)),
        out_shape=jax.ShapeDtypeStruct((b, l, V_B), BF16),
        scratch_shapes=[pltpu.VMEM((N_HEADS_B // 2, KEY_DIM_B, 2 * VAL_DIM_B), F32)],
        compiler_params=_cparams(("parallel", "arbitrary")),
    )(hc, hc, hc, kt, hc, small, a_log, dt_bias, norm_g)


def _merge_kernel(x_ref, ya_ref, yb_ref, ga_ref, gb_ref, wa_ref, wb_ref, wo_ref, g_ref, b_ref, o_ref, *, alpha):
    ya = jnp.dot(ya_ref[...], wa_ref[...], preferred_element_type=F32)
    yb = jnp.dot(yb_ref[...], wb_ref[...], preferred_element_type=F32)
    mix = ga_ref[...] * ya + gb_ref[...] * yb
    y = jnp.dot(mix.astype(BF16), wo_ref[...], preferred_element_type=F32)
    o_ref[...] = _layernorm(alpha * x_ref[...] + y, g_ref[...], b_ref[...])


def _merge(x2, ya, yb, hc2, wa, wb, wo, g, b, *, alpha, tm):
    t, d = x2.shape
    tile = lambda cidx=0: pl.BlockSpec((tm, d), lambda i: (i, cidx))
    return pl.pallas_call(
        functools.partial(_merge_kernel, alpha=alpha),
        grid=(t // tm,),
        in_specs=[tile(), tile(), tile(), tile(COL_GA // Q_A), tile(COL_GA // Q_A + 1),
                  _resident(wa.shape), _resident(wb.shape), _resident(wo.shape),
                  _resident(g.shape), _resident(b.shape)],
        out_specs=tile(),
        out_shape=jax.ShapeDtypeStruct((t, d), F32),
        compiler_params=_cparams(("parallel",)),
    )(x2, ya, yb, hc2, hc2, wa, wb, wo, g, b)


def _prep_w_in(w_in):
    d = w_in.shape[0]
    narrow = W_Z
    w_main = jnp.concatenate([w_in[:, :Q_A] * (HEAD_DIM_A ** -0.5), w_in[:, Q_A:narrow],
                              w_in[:, narrow + 2 * N_HEADS_B:]], axis=1).astype(BF16)
    w_small = jnp.concatenate([w_in[:, narrow:narrow + 2 * N_HEADS_B],
                               jnp.zeros((d, LANES - 2 * N_HEADS_B), w_in.dtype)], axis=1).astype(BF16)
    return w_main, w_small


def _prep_ffn(w13, w2):
    f = w13.shape[1] // 2
    return w13[:, :f].astype(BF16), w13[:, f:].astype(BF16), w2.astype(BF16)


def kernel(x, rel_bias, ln_g, ln_b, ffn_w13, ffn_w2, w_in, conv_w, a_log, dt_bias, dn_norm_g, sinks,
           w_branch_a, w_branch_b, w_out):
    b, l, d = x.shape
    depth = w_in.shape[0]
    alpha = (2 * depth) ** 0.25
    t = b * l
    tm = 512
    pos_bias = _pos_bias(rel_bias)
    x2 = x.reshape(t, d)
    row = lambda v: v.reshape(1, -1)
    colv = lambda v: v.reshape(-1, 1)
    for i in range(depth):
        x2 = _ffn_ln(x2, *_prep_ffn(ffn_w13[i, 0], ffn_w2[i, 0]), row(ln_g[i, 0]), row(ln_b[i, 0]),
                     alpha=alpha, tm=2 * tm)
        w_main, w_small = _prep_w_in(w_in[i])
        hc, kt, small = _in_proj(x2.reshape(b, l, d), w_main, w_small, conv_w[i], tm=tm)
        ya = _attention(hc, pos_bias, sinks[i])
        yb = _delta(hc, kt, small, colv(a_log[i]), colv(dt_bias[i]), row(dn_norm_g[i]), n_chunks=4)
        x2 = _merge(x2, ya.reshape(t, Q_A), yb.reshape(t, V_B), hc.reshape(t, N_MAIN),
                    w_branch_a[i].astype(BF16), w_branch_b[i].astype(BF16), w_out[i].astype(BF16),
                    row(ln_g[i, 1]), row(ln_b[i, 1]), alpha=alpha, tm=tm)
        x2 = _ffn_ln(x2, *_prep_ffn(ffn_w13[i, 1], ffn_w2[i, 1]), row(ln_g[i, 2]), row(ln_b[i, 2]),
                     alpha=alpha, tm=2 * tm)
    return x2.reshape(b, l, d)
```

```python
import functools
import math

import jax
import jax.numpy as jnp
from jax import lax
from jax.experimental import pallas as pl
from jax.experimental.pallas import tpu as pltpu

BF16 = jnp.bfloat16
F32 = jnp.float32

N_HEADS_A = 16
N_KV_A = 4
HEAD_DIM_A = 64
WINDOW = 128
N_HEADS_B = 8
KEY_DIM_B = 128
VAL_DIM_B = 128
CONV_K = 4
NUM_BUCKETS = 32
MAX_DISTANCE = 128
LOG2E = 1.4426950408889634
LN_EPS = 1e-5
NORM_EPS = 1e-6
NEG_INF = -1e30

LANES = 128
SUBLANES = 8
VMEM_LIMIT_BYTES = 58 * 1024 * 1024

Q_A = N_HEADS_A * HEAD_DIM_A
KDUP = N_KV_A * 2 * HEAD_DIM_A
QK_B = N_HEADS_B * KEY_DIM_B
V_B = N_HEADS_B * VAL_DIM_B
COL_QA = 0
COL_KA = COL_QA + Q_A
COL_VA = COL_KA + KDUP
COL_QB = COL_VA + KDUP
COL_KB = COL_QB + QK_B
COL_VB = COL_KB + QK_B
COL_Z = COL_VB + V_B
COL_GA = COL_Z + V_B
N_MAIN = COL_GA + 2 * Q_A

KV_A = N_KV_A * HEAD_DIM_A
W_KV = Q_A
W_QB = W_KV + 2 * KV_A
W_Z = W_QB + 2 * QK_B + V_B
W_GA = W_Z + V_B
W_MAIN = W_GA + 2 * Q_A

DELTA_CHUNK = 128
FFN_CHUNK = 256
PROJ_CHUNK = 512
ATTN_BLOCKS = 2


def _cparams(sem):
    return pltpu.CompilerParams(dimension_semantics=sem, vmem_limit_bytes=VMEM_LIMIT_BYTES)


def _resident(shape):
    nd = len(shape)
    return pl.BlockSpec(shape, lambda *_: (0,) * nd, pipeline_mode=pl.Buffered(1))


def _layernorm(y, g, b):
    mu = jnp.mean(y, axis=-1, keepdims=True)
    d = y - mu
    var = jnp.mean(d * d, axis=-1, keepdims=True)
    return d * lax.rsqrt(var + LN_EPS) * g + b


def _silu(x):
    return x * jax.nn.sigmoid(x)


def _ffn_ln_kernel(x_ref, wg_ref, wu_ref, w2_ref, g_ref, b_ref, o_ref, *, alpha):
    half = x_ref.shape[0] // 2
    for part in range(2):
        rows = slice(part * half, (part + 1) * half)
        x = x_ref[rows, :]
        xb = x.astype(BF16)
        acc = jnp.zeros(x.shape, F32)
        for c in range(wg_ref.shape[1] // FFN_CHUNK):
            cols = slice(c * FFN_CHUNK, (c + 1) * FFN_CHUNK)
            h = jnp.dot(xb, wg_ref[:, cols], preferred_element_type=F32)
            u = jnp.dot(xb, wu_ref[:, cols], preferred_element_type=F32)
            a = (_silu(h) * u).astype(BF16)
            acc = acc + jnp.dot(a, w2_ref[cols, :], preferred_element_type=F32)
        o_ref[rows, :] = _layernorm(alpha * x + 0.5 * acc, g_ref[...], b_ref[...])


def _ffn_ln(x2, wg, wu, w2, g, b, *, alpha, tm):
    t, d = x2.shape
    return pl.pallas_call(
        functools.partial(_ffn_ln_kernel, alpha=alpha),
        grid=(t // tm,),
        in_specs=[pl.BlockSpec((tm, d), lambda i: (i, 0)),
                  _resident(wg.shape), _resident(wu.shape), _resident(w2.shape),
                  _resident(g.shape), _resident(b.shape)],
        out_specs=pl.BlockSpec((tm, d), lambda i: (i, 0)),
        out_shape=jax.ShapeDtypeStruct((t, d), F32),
        compiler_params=_cparams(("parallel",)),
    )(x2, wg, wu, w2, g, b)


def _in_proj_kernel(x_ref, w_ref, ws_ref, cw_ref, hc_ref, kt_ref, sm_ref, raw_ref):
    tm = x_ref.shape[1]
    halo = SUBLANES
    per_chunk = PROJ_CHUNK // LANES
    n_conv = (COL_Z - COL_QB) // LANES

    @pl.when(pl.program_id(1) == 0)
    def _():
        raw_ref[:, 0:halo, :] = jnp.zeros((n_conv, halo, LANES), F32)

    xb = x_ref[0].astype(BF16)

    def proj(c):
        return jnp.dot(xb, w_ref[:, c * PROJ_CHUNK:(c + 1) * PROJ_CHUNK], preferred_element_type=F32)

    def conv_block(cb):
        acc = None
        for j in range(CONV_K):
            w = cw_ref[j:j + 1, cb * LANES:(cb + 1) * LANES]
            term = raw_ref[cb, halo - (CONV_K - 1) + j: halo - (CONV_K - 1) + j + tm, :] * w
            acc = term if acc is None else acc + term
        y = _silu(acc)
        kind = cb // N_HEADS_B
        if kind < 2:
            y = y * lax.rsqrt(jnp.sum(y * y, axis=-1, keepdims=True) + NORM_EPS)
        if kind == 0:
            y = y * (KEY_DIM_B ** -0.5)
        hc_ref[0, :, COL_QB + cb * LANES:COL_QB + (cb + 1) * LANES] = y.astype(BF16)
        if kind == 1:
            h = cb - N_HEADS_B
            kt_ref[0, h * LANES:(h + 1) * LANES, :] = y.T.astype(BF16)

    low = lax.broadcasted_iota(jnp.int32, (tm, LANES), 1) < HEAD_DIM_A

    def store_twice(r, col):
        for i in range(KV_A // LANES):
            two = r[:, i * LANES:(i + 1) * LANES]
            swapped = pltpu.roll(two, HEAD_DIM_A, 1)
            hc_ref[0, :, col + 2 * i * LANES:col + (2 * i + 1) * LANES] = jnp.where(low, two, swapped).astype(BF16)
            hc_ref[0, :, col + (2 * i + 1) * LANES:col + (2 * i + 2) * LANES] = jnp.where(low, swapped, two).astype(BF16)

    n_proj = W_MAIN // PROJ_CHUNK
    heavy = [c for c in range(n_proj) if W_QB <= c * PROJ_CHUNK < W_Z]
    light = [c for c in range(n_proj) if c not in heavy]
    order = []
    while heavy or light:
        if heavy:
            order.append(heavy.pop(0))
        if light:
            order.append(light.pop(0))
    for c in order:
        lo = c * PROJ_CHUNK
        r = proj(c)
        if lo < W_KV:
            hc_ref[0, :, COL_QA + lo:COL_QA + lo + PROJ_CHUNK] = r.astype(BF16)
        elif lo < W_QB:
            store_twice(r[:, :KV_A], COL_KA)
            store_twice(r[:, KV_A:], COL_VA)
        elif lo < W_Z:
            first = (lo - W_QB) // LANES
            for i in range(per_chunk):
                raw_ref[first + i, halo:halo + tm, :] = r[:, i * LANES:(i + 1) * LANES]
            for i in range(per_chunk):
                conv_block(first + i)
        elif lo < W_GA:
            hc_ref[0, :, COL_Z + lo - W_Z:COL_Z + lo - W_Z + PROJ_CHUNK] = _silu(r).astype(BF16)
        else:
            hc_ref[0, :, COL_GA + lo - W_GA:COL_GA + lo - W_GA + PROJ_CHUNK] = r.astype(BF16)
    small = jnp.dot(xb, ws_ref[...], preferred_element_type=F32)
    sm_ref[0] = small.T[:2 * N_HEADS_B, :]
    raw_ref[:, 0:halo, :] = raw_ref[:, tm:tm + halo, :]


def _in_proj(x3, w_main, w_small, conv_w, *, tm):
    b, l, d = x3.shape
    n_conv = (COL_Z - COL_QB) // LANES
    return pl.pallas_call(
        _in_proj_kernel,
        grid=(b, l // tm),
        in_specs=[pl.BlockSpec((1, tm, d), lambda i, j: (i, j, 0)),
                  _resident(w_main.shape), _resident(w_small.shape), _resident(conv_w.shape)],
        out_specs=[pl.BlockSpec((1, tm, N_MAIN), lambda i, j: (i, j, 0)),
                   pl.BlockSpec((1, QK_B, tm), lambda i, j: (i, 0, j)),
                   pl.BlockSpec((1, 2 * N_HEADS_B, tm), lambda i, j: (i, 0, j))],
        out_shape=[jax.ShapeDtypeStruct((b, l, N_MAIN), BF16),
                   jax.ShapeDtypeStruct((b, QK_B, l), BF16),
                   jax.ShapeDtypeStruct((b, 2 * N_HEADS_B, l), F32)],
        scratch_shapes=[pltpu.VMEM((n_conv, tm + SUBLANES, LANES), F32)],
        compiler_params=_cparams(("parallel", "arbitrary")),
    )(x3, w_main, w_small, conv_w)


def _t5_bucket(rel):
    n = jnp.maximum(rel, 0)
    max_exact = NUM_BUCKETS // 2
    nf = jnp.maximum(n, 1).astype(F32)
    large = max_exact + (jnp.log(nf / max_exact) / math.log(MAX_DISTANCE / max_exact)
                         * (NUM_BUCKETS - max_exact)).astype(jnp.int32)
    large = jnp.minimum(large, NUM_BUCKETS - 1)
    return jnp.where(n < max_exact, n, large)


def _pos_bias_kernel(rb_ref, bucket_ref, rel_ref, o_ref):
    h = pl.program_id(0)
    bucket = bucket_ref[...]
    rel = rel_ref[...]
    bias = jnp.zeros(bucket.shape, F32)
    for bk in range(NUM_BUCKETS):
        bias = jnp.where(bucket == bk, rb_ref[bk, h], bias)
    in_band = (rel >= 0) & (rel < WINDOW)
    own = lax.broadcasted_iota(jnp.int32, rel.shape, 1) >= WINDOW
    bias = bias * LOG2E
    o_ref[0, 0] = jnp.where(in_band & own, bias, NEG_INF)
    o_ref[1, 0] = jnp.where(in_band, bias, NEG_INF)


def _pos_bias(rel_bias):
    r = jnp.arange(WINDOW, dtype=jnp.int32)[:, None]
    j = jnp.arange(2 * WINDOW, dtype=jnp.int32)[None, :]
    rel = r + WINDOW - j
    bucket = _t5_bucket(rel).astype(jnp.int32)
    full = pl.BlockSpec((WINDOW, 2 * WINDOW), lambda h: (0, 0))
    return pl.pallas_call(
        _pos_bias_kernel,
        grid=(N_HEADS_A,),
        in_specs=[pl.BlockSpec(memory_space=pltpu.SMEM), full, full],
        out_specs=pl.BlockSpec((2, 1, WINDOW, 2 * WINDOW), lambda h: (0, h, 0, 0)),
        out_shape=jax.ShapeDtypeStruct((2, N_HEADS_A, WINDOW, 2 * WINDOW), F32),
        compiler_params=_cparams(("arbitrary",)),
    )(rel_bias, bucket, rel)


def _attn_kernel(sink_ref, q_ref, kc_ref, kp_ref, vc_ref, vp_ref, bias0_ref, bias1_ref, o_ref):
    grp = N_HEADS_A // N_KV_A
    items = [(qb, g) for qb in range(ATTN_BLOCKS) for g in range(N_KV_A)]
    low = lax.broadcasted_iota(jnp.int32, (WINDOW, LANES), 1) < HEAD_DIM_A
    zero = jnp.zeros((WINDOW, LANES), BF16)
    ones = jnp.ones((2 * WINDOW, LANES), BF16)
    nt = (((1,), (1,)), ((), ()))

    def q_stack(qb, g):
        rows = slice(qb * WINDOW, (qb + 1) * WINDOW)
        parts = []
        for pair in range(grp // 2):
            qp = q_ref[0, rows, pl.ds((g * grp + 2 * pair) * HEAD_DIM_A, LANES)]
            parts += [jnp.where(low, qp, zero), jnp.where(low, zero, qp)]
        return jnp.concatenate(parts, axis=0)

    def kv_cat(prev_ref, cur_ref, qb, g):
        cols = slice(g * LANES, (g + 1) * LANES)
        if qb == 0:
            return jnp.concatenate([prev_ref[0, :, cols], cur_ref[0, 0:WINDOW, cols]], axis=0)
        return cur_ref[0, (qb - 1) * WINDOW:(qb + 1) * WINDOW, cols]

    def bias(qb, g):
        ref = bias0_ref if qb == 0 else bias1_ref
        return ref[0, g * grp:(g + 1) * grp].reshape(grp * WINDOW, 2 * WINDOW)

    sinks = [jnp.concatenate([jnp.full((WINDOW, LANES), sink_ref[g * grp + h] * LOG2E, F32) for h in range(grp)], axis=0)
             for g in range(N_KV_A)]
    q4s = [q_stack(qb, g) for qb, g in items]
    kks = [kv_cat(kp_ref, kc_ref, qb, g) for qb, g in items]
    vxs = [jnp.concatenate([kv_cat(vp_ref, vc_ref, qb, g), ones], axis=1) for qb, g in items]
    ss = [lax.dot_general(q4, kk, nt, preferred_element_type=F32) + bias(qb, g)
          for (qb, g), q4, kk in zip(items, q4s, kks)]
    ms = [jnp.maximum(jnp.broadcast_to(jnp.max(jnp.maximum(s[:, :LANES], s[:, LANES:]), axis=-1, keepdims=True),
                                       (grp * WINDOW, LANES)), sinks[g]) for (_, g), s in zip(items, ss)]
    ps = [jnp.exp2(s - jnp.concatenate([m, m], axis=1)).astype(BF16) for s, m in zip(ss, ms)]
    ols = [jnp.dot(p, vx, preferred_element_type=F32) for p, vx in zip(ps, vxs)]
    for (qb, g), ol, m in zip(items, ols, ms):
        o = ol[:, :LANES] / (ol[:, LANES:] + jnp.exp2(sinks[g] - m))
        for pair in range(grp // 2):
            r0 = (2 * pair) * WINDOW
            both = jnp.where(low, o[r0:r0 + WINDOW], o[r0 + WINDOW:r0 + 2 * WINDOW])
            o_ref[0, qb * WINDOW:(qb + 1) * WINDOW, pl.ds((g * grp + 2 * pair) * HEAD_DIM_A, LANES)] = both.astype(BF16)


def _attention(hc, pos_bias, sink):
    b, l, _ = hc.shape
    rows = ATTN_BLOCKS * WINDOW
    kblk, vblk = COL_KA // KDUP, COL_VA // KDUP
    prev = lambda i, j: (i, jnp.maximum(ATTN_BLOCKS * j - 1, 0))
    bias_spec = lambda f: pl.BlockSpec((1, N_HEADS_A, WINDOW, 2 * WINDOW), f)
    return pl.pallas_call(
        _attn_kernel,
        grid=(b, l // rows),
        in_specs=[pl.BlockSpec(memory_space=pltpu.SMEM),
                  pl.BlockSpec((1, rows, Q_A), lambda i, j: (i, j, COL_QA // Q_A)),
                  pl.BlockSpec((1, rows, KDUP), lambda i, j: (i, j, kblk)),
                  pl.BlockSpec((1, WINDOW, KDUP), lambda i, j: (*prev(i, j), kblk)),
                  pl.BlockSpec((1, rows, KDUP), lambda i, j: (i, j, vblk)),
                  pl.BlockSpec((1, WINDOW, KDUP), lambda i, j: (*prev(i, j), vblk)),
                  bias_spec(lambda i, j: (jnp.minimum(j, 1), 0, 0, 0)),
                  bias_spec(lambda i, j: (1, 0, 0, 0))],
        out_specs=pl.BlockSpec((1, rows, Q_A), lambda i, j: (i, j, 0)),
        out_shape=jax.ShapeDtypeStruct((b, l, Q_A), BF16),
        compiler_params=_cparams(("parallel", "arbitrary")),
    )(sink, hc, hc, hc, hc, hc, pos_bias, pos_bias)


def _chunk_sums(g):
    c = DELTA_CHUNK
    n = g.shape[1] // c
    hi = g.astype(BF16).astype(F32)
    r1 = g - hi
    mid = r1.astype(BF16).astype(F32)
    lo = r1 - mid
    s = lax.broadcasted_iota(jnp.int32, (c, 2 * c), 0)
    t = lax.broadcasted_iota(jnp.int32, (c, 2 * c), 1)
    tri = jnp.where(((t < c) & (s <= t)) | ((t >= c) & (s > t - c)), 1.0, 0.0).astype(BF16)
    lhs = jnp.concatenate([piece[:, ci * c:(ci + 1) * c] for ci in range(n) for piece in (hi, mid, lo)],
                          axis=0).astype(BF16)
    out = jnp.dot(lhs, tri, preferred_element_type=F32)
    k = g.shape[0]
    both = [out[3 * k * ci:3 * k * ci + k] + out[3 * k * ci + k:3 * k * ci + 2 * k]
            + out[3 * k * ci + 2 * k:3 * k * (ci + 1)] for ci in range(n)]
    prefix = jnp.concatenate([bt[:, :c] for bt in both], axis=1)
    suffix = jnp.concatenate([bt[:, c:] for bt in both], axis=1)
    return prefix, suffix


def _halves(t):
    return t[:, :LANES], t[:, LANES:]


def _pair(a, b):
    return jnp.concatenate([a, b], axis=1)


def _block_diag(t):
    a, b = _halves(t)
    z = jnp.zeros_like(a)
    return jnp.concatenate([_pair(a, z), _pair(z, b)], axis=0)


def _mm2(a, b):
    return jnp.dot(a.astype(BF16), _block_diag(b.astype(BF16)), preferred_element_type=F32)


def _scatter_rows(t, stripes, n):
    parts, at, used = [], 0, 0
    for s in stripes:
        if s.start > at:
            parts.append(jnp.zeros((s.start - at, t.shape[1]), t.dtype))
        size = s.stop - s.start
        parts.append(t[used:used + size])
        used += size
        at = s.stop
    if at < n:
        parts.append(jnp.zeros((n - at, t.shape[1]), t.dtype))
    return jnp.concatenate(parts, axis=0)


def _unit_lower_inverses(lms, row, col):
    c = row.shape[0]
    eye = (row == col).astype(F32)
    base = SUBLANES
    same = lambda size: (row & -size) == (col & -size)
    lds = [jnp.where(same(base), lm, 0.0) for lm in lms]
    a1s = [eye - ld for ld in lds]
    ldbs = [ld.astype(BF16) for ld in lds]
    ld2bs = [_mm2(ldb, ldb).astype(BF16) for ldb in ldbs]
    ld4s = [_mm2(ld2b, ld2b) for ld2b in ld2bs]
    p1s = [a1 + _mm2(a1, ld2b) for a1, ld2b in zip(a1s, ld2bs)]
    xs = [p1 + _mm2(p1, ld4) for p1, ld4 in zip(p1s, ld4s)]
    lmbs = [lm.astype(BF16) for lm in lms]
    zero = jnp.zeros(row.shape, BF16)
    blk = base
    while blk < c:
        mask = same(2 * blk) & jnp.logical_not(same(blk))
        xbs = [x.astype(BF16) for x in xs]
        if blk % (2 * SUBLANES) == 0:
            stripes = [slice(r0, r0 + blk) for r0 in range(blk, c, 2 * blk)]
            rows_of = lambda t: jnp.concatenate([t[s] for s in stripes], axis=0)
            exs = [_mm2(rows_of(jnp.where(mask, lmb, zero)), xb) for lmb, xb in zip(lmbs, xbs)]
            upd = [_mm2(rows_of(xb), _scatter_rows(ex, stripes, c)) for xb, ex in zip(xbs, exs)]
            xs = [x - _scatter_rows(u, stripes, c) for x, u in zip(xs, upd)]
        else:
            exs = [_mm2(jnp.where(mask, lmb, zero), xb) for lmb, xb in zip(lmbs, xbs)]
            xs = [x - _mm2(xb, ex) for x, xb, ex in zip(xs, xbs, exs)]
        blk *= 2
    return xs


def _delta_kernel(q_ref, k_ref, v_ref, kt_ref, z_ref, sm_ref, alog_ref, dtb_ref, ng_ref, o_ref, s_ref, *, n_chunks):
    c = DELTA_CHUNK
    blk = n_chunks * c

    @pl.when(pl.program_id(1) == 0)
    def _():
        s_ref[...] = jnp.zeros(s_ref.shape, F32)

    sm = sm_ref[0]
    beta = jax.nn.sigmoid(sm[0:N_HEADS_B])
    log_beta = -jax.nn.softplus(-sm[0:N_HEADS_B])
    glog = -jnp.exp(alog_ref[...]) * jax.nn.softplus(sm[N_HEADS_B:2 * N_HEADS_B] + dtb_ref[...])
    gc, tail = _chunk_sums(glog)
    etail = jnp.exp(tail)
    egc = jnp.exp(gc)
    beta_egc = beta * egc
    rows = jnp.concatenate([gc + log_beta, gc, egc, jnp.exp(gc + tail),
                            jnp.zeros((LANES - 4 * N_HEADS_B, blk), F32)], axis=0)
    cols = rows.T

    n_pairs = N_HEADS_B // 2
    wide = 2 * LANES
    row_i = lax.broadcasted_iota(jnp.int32, (c, wide), 0)
    col_i = lax.broadcasted_iota(jnp.int32, (c, wide), 1) & (LANES - 1)
    causal = row_i >= col_i
    strict = row_i > col_i

    def col2(kind, p, rows_):
        base = kind * N_HEADS_B + 2 * p
        return _pair(jnp.broadcast_to(cols[rows_, base:base + 1], (rows_.stop - rows_.start, LANES)),
                     jnp.broadcast_to(cols[rows_, base + 1:base + 2], (rows_.stop - rows_.start, LANES)))

    def row2(arr, p, t):
        return _pair(arr[2 * p:2 * p + 1, t], arr[2 * p + 1:2 * p + 2, t])

    items = [(p, ci) for ci in range(n_chunks) for p in range(n_pairs)]
    pt = [(p, slice(ci * c, (ci + 1) * c)) for p, ci in items]
    lanes_of = lambda p: slice(p * wide, (p + 1) * wide)
    qs = [q_ref[0, t, lanes_of(p)] for p, t in pt]
    ks = [k_ref[0, t, lanes_of(p)] for p, t in pt]
    vs = [v_ref[0, t, lanes_of(p)] for p, t in pt]
    kts = [_pair(kt_ref[0, 2 * p * LANES:(2 * p + 1) * LANES, t],
                 kt_ref[0, (2 * p + 1) * LANES:(2 * p + 2) * LANES, t]) for p, t in pt]
    gcbs = [col2(0, p, t) for p, t in pt]
    gccs = [col2(1, p, t) for p, t in pt]
    egcs = [col2(2, p, t) for p, t in pt]
    gcrs = [row2(gc, p, t) for p, t in pt]
    etrs = [row2(etail, p, t) for p, t in pt]
    brs = [row2(beta, p, t) for p, t in pt]
    bers = [row2(beta_egc, p, t) for p, t in pt]

    prods = [jnp.dot(jnp.concatenate([k, q], axis=0), _block_diag(kt), preferred_element_type=F32)
             for k, q, kt in zip(ks, qs, kts)]
    lms = [pr[:c] * jnp.exp(jnp.where(strict, gcb - gcr, NEG_INF)) for pr, gcb, gcr in zip(prods, gcbs, gcrs)]
    a_intras = [pr[c:] * jnp.exp(jnp.where(causal, gcc - gcr, NEG_INF)) for pr, gcc, gcr in zip(prods, gccs, gcrs)]
    t_invs = _unit_lower_inverses(lms, row_i, col_i)
    us = [jnp.dot((t_inv * br).astype(BF16), _block_diag(v), preferred_element_type=F32)
          for t_inv, br, v in zip(t_invs, brs, vs)]
    ws = [jnp.dot((t_inv * ber).astype(BF16), _block_diag(k), preferred_element_type=F32)
          for t_inv, ber, k in zip(t_invs, bers, ks)]
    wqs = [jnp.concatenate([w.astype(BF16), (q * egc).astype(BF16)], axis=0)
           for w, q, egc in zip(ws, qs, egcs)]
    akts = [jnp.concatenate([a.astype(BF16), (kt * etr).astype(BF16)], axis=0)
            for a, kt, etr in zip(a_intras, kts, etrs)]

    states = [s_ref[p] for p in range(n_pairs)]
    for ci in range(n_chunks):
        idx = [ci * n_pairs + p for p in range(n_pairs)]
        tok = slice(ci * c, (ci + 1) * c)
        r1 = [jnp.dot(wqs[i], _block_diag(states[p].astype(BF16)), preferred_element_type=F32)
              for p, i in enumerate(idx)]
        v_news = [us[i] - r[:c] for i, r in zip(idx, r1)]
        r2 = [jnp.dot(akts[i], _block_diag(vn.astype(BF16)), preferred_element_type=F32)
              for i, vn in zip(idx, v_news)]
        for p in range(n_pairs):
            base = 3 * N_HEADS_B + 2 * p
            egl = _pair(jnp.broadcast_to(cols[ci * c:ci * c + 1, base:base + 1], (1, LANES)),
                        jnp.broadcast_to(cols[ci * c:ci * c + 1, base + 1:base + 2], (1, LANES)))
            states[p] = states[p] * egl + r2[p][c:]
            o = r1[p][c:] + r2[p][:c]
            o = _pair(*[u * lax.rsqrt(jnp.mean(u * u, axis=-1, keepdims=True) + NORM_EPS) * ng_ref[...]
                        for u in _halves(o)])
            o_ref[0, tok, lanes_of(p)] = (o * z_ref[0, tok, lanes_of(p)]).astype(BF16)
    for p in range(n_pairs):
        s_ref[p] = states[p]


def _delta(hc, kt, small, a_log, dt_bias, norm_g, *, n_chunks):
    b, l, _ = hc.shape
    blk = n_chunks * DELTA_CHUNK
    col = lambda cidx: pl.BlockSpec((1, blk, QK_B), lambda i, j: (i, j, cidx))
    return pl.pallas_call(
        functools.partial(_delta_kernel, n_chunks=n_chunks),
        grid=(b, l // blk),
        in_specs=[col(COL_QB // QK_B), col(COL_KB // QK_B), col(COL_VB // QK_B),
                  pl.BlockSpec((1, QK_B, blk), lambda i, j: (i, 0, j)),
                  col(COL_Z // QK_B),
                  pl.BlockSpec((1, 2 * N_HEADS_B, blk), lambda i, j: (i, 0, j)),
                  _resident(a_log.shape), _resident(dt_bias.shape), _resident(norm_g.shape)],
        out_specs=pl.BlockSpec((1, blk, V_B), lambda i, j: (i, j, 0)),
        out_shape=jax.ShapeDtypeStruct((b, l, V_B), BF16),
        scratch_shapes=[pltpu.VMEM((N_HEADS_B // 2, KEY_DIM_B, 2 * VAL_DIM_B), F32)],
        compiler_params=_cparams(("parallel", "arbitrary")),
    )(hc, hc, hc, kt, hc, small, a_log, dt_bias, norm_g)


def _merge_kernel(x_ref, ya_ref, yb_ref, ga_ref, gb_ref, wa_ref, wb_ref, wo_ref, g_ref, b_ref, o_ref, *, alpha):
    half = x_ref.shape[0] // 2
    for part in range(2):
        rows = slice(part * half, (part + 1) * half)
        ya = jnp.dot(ya_ref[rows, :], wa_ref[...], preferred_element_type=F32)
        yb = jnp.dot(yb_ref[rows, :], wb_ref[...], preferred_element_type=F32)
        mix = (jax.nn.sigmoid(ga_ref[rows, :].astype(F32)) * ya
               + jax.nn.sigmoid(gb_ref[rows, :].astype(F32)) * yb)
        y = jnp.dot(mix.astype(BF16), wo_ref[...], preferred_element_type=F32)
        o_ref[rows, :] = _layernorm(alpha * x_ref[rows, :] + y, g_ref[...], b_ref[...])


def _merge(x2, ya, yb, hc2, wa, wb, wo, g, b, *, alpha, tm):
    t, d = x2.shape
    tile = lambda cidx=0: pl.BlockSpec((tm, d), lambda i: (i, cidx))
    return pl.pallas_call(
        functools.partial(_merge_kernel, alpha=alpha),
        grid=(t // tm,),
        in_specs=[tile(), tile(), tile(), tile(COL_GA // Q_A), tile(COL_GA // Q_A + 1),
                  _resident(wa.shape), _resident(wb.shape), _resident(wo.shape),
                  _resident(g.shape), _resident(b.shape)],
        out_specs=tile(),
        out_shape=jax.ShapeDtypeStruct((t, d), F32),
        compiler_params=_cparams(("parallel",)),
    )(x2, ya, yb, hc2, hc2, wa, wb, wo, g, b)


def _prep_w_in(w_in):
    d = w_in.shape[0]
    narrow = W_Z
    w_main = jnp.concatenate([w_in[:, :Q_A] * (HEAD_DIM_A ** -0.5 * LOG2E), w_in[:, Q_A:narrow],
                              w_in[:, narrow + 2 * N_HEADS_B:]], axis=1).astype(BF16)
    w_small = jnp.concatenate([w_in[:, narrow:narrow + 2 * N_HEADS_B],
                               jnp.zeros((d, LANES - 2 * N_HEADS_B), w_in.dtype)], axis=1).astype(BF16)
    return w_main, w_small


def _prep_ffn(w13, w2):
    f = w13.shape[1] // 2
    return w13[:, :f].astype(BF16), w13[:, f:].astype(BF16), w2.astype(BF16)


def kernel(x, rel_bias, ln_g, ln_b, ffn_w13, ffn_w2, w_in, conv_w, a_log, dt_bias, dn_norm_g, sinks,
           w_branch_a, w_branch_b, w_out):
    b, l, d = x.shape
    depth = w_in.shape[0]
    alpha = (2 * depth) ** 0.25
    t = b * l
    tm = 512
    pos_bias = _pos_bias(rel_bias)
    x2 = x.reshape(t, d)
    row = lambda v: v.reshape(1, -1)
    colv = lambda v: v.reshape(-1, 1)
    for i in range(depth):
        x2 = _ffn_ln(x2, *_prep_ffn(ffn_w13[i, 0], ffn_w2[i, 0]), row(ln_g[i, 0]), row(ln_b[i, 0]),
                     alpha=alpha, tm=2 * tm)
        w_main, w_small = _prep_w_in(w_in[i])
        hc, kt, small = _in_proj(x2.reshape(b, l, d), w_main, w_small, conv_w[i], tm=tm)
        ya = _attention(hc, pos_bias, sinks[i])
        yb = _delta(hc, kt, small, colv(a_log[i]), colv(dt_bias[i]), row(dn_norm_g[i]), n_chunks=4)
        x2 = _merge(x2, ya.reshape(t, Q_A), yb.reshape(t, V_B), hc.reshape(t, N_MAIN),
                    w_branch_a[i].astype(BF16), w_branch_b[i].astype(BF16), w_out[i].astype(BF16),
                    row(ln_g[i, 1]), row(ln_b[i, 1]), alpha=alpha, tm=2 * tm)
        x2 = _ffn_ln(x2, *_prep_ffn(ffn_w13[i, 1], ffn_w2[i, 1]), row(ln_g[i, 2]), row(ln_b[i, 2]),
                     alpha=alpha, tm=2 * tm)
    return x2.reshape(b, l, d)
```

```python
import functools
import math

import jax
import jax.numpy as jnp
from jax import lax
from jax.experimental import pallas as pl
from jax.experimental.pallas import tpu as pltpu

BF16 = jnp.bfloat16
F32 = jnp.float32

N_HEADS_A = 16
N_KV_A = 4
HEAD_DIM_A = 64
WINDOW = 128
N_HEADS_B = 8
KEY_DIM_B = 128
VAL_DIM_B = 128
CONV_K = 4
NUM_BUCKETS = 32
MAX_DISTANCE = 128
LOG2E = 1.4426950408889634
LN_EPS = 1e-5
NORM_EPS = 1e-6
NEG_INF = -1e30

LANES = 128
SUBLANES = 8
VMEM_LIMIT_BYTES = 58 * 1024 * 1024

Q_A = N_HEADS_A * HEAD_DIM_A
KDUP = N_KV_A * 2 * HEAD_DIM_A
QK_B = N_HEADS_B * KEY_DIM_B
V_B = N_HEADS_B * VAL_DIM_B
COL_QA = 0
COL_KA = COL_QA + Q_A
COL_VA = COL_KA + KDUP
COL_QB = COL_VA + KDUP
COL_KB = COL_QB + QK_B
COL_VB = COL_KB + QK_B
COL_Z = COL_VB + V_B
COL_GA = COL_Z + V_B
N_MAIN = COL_GA + 2 * Q_A

KV_A = N_KV_A * HEAD_DIM_A
W_KV = Q_A
W_QB = W_KV + 2 * KV_A
W_Z = W_QB + 2 * QK_B + V_B
W_GA = W_Z + V_B
W_MAIN = W_GA + 2 * Q_A

DELTA_CHUNK = 128
FFN_CHUNK = 256
PROJ_CHUNK = 512
DELTA_GROUP = 2
ATTN_BLOCKS = 4


def _cparams(sem):
    return pltpu.CompilerParams(dimension_semantics=sem, vmem_limit_bytes=VMEM_LIMIT_BYTES)


def _resident(shape):
    nd = len(shape)
    return pl.BlockSpec(shape, lambda *_: (0,) * nd, pipeline_mode=pl.Buffered(1))


def _layernorm(y, g, b):
    mu = jnp.mean(y, axis=-1, keepdims=True)
    d = y - mu
    var = jnp.mean(d * d, axis=-1, keepdims=True)
    return d * lax.rsqrt(var + LN_EPS) * g + b


def _silu(x):
    return x * jax.nn.sigmoid(x)


def _ffn_ln_kernel(x_ref, wg_ref, wu_ref, w2_ref, g_ref, b_ref, o_ref, *, alpha):
    half = x_ref.shape[0] // 2
    for part in range(2):
        rows = slice(part * half, (part + 1) * half)
        x = x_ref[rows, :]
        xb = x.astype(BF16)
        acc = jnp.zeros(x.shape, F32)
        for c in range(wg_ref.shape[1] // FFN_CHUNK):
            cols = slice(c * FFN_CHUNK, (c + 1) * FFN_CHUNK)
            h = jnp.dot(xb, wg_ref[:, cols], preferred_element_type=F32)
            u = jnp.dot(xb, wu_ref[:, cols], preferred_element_type=F32)
            a = (_silu(h) * u).astype(BF16)
            acc = acc + jnp.dot(a, w2_ref[cols, :], preferred_element_type=F32)
        o_ref[rows, :] = _layernorm(alpha * x + 0.5 * acc, g_ref[...], b_ref[...])


def _ffn_ln(x2, wg, wu, w2, g, b, *, alpha, tm):
    t, d = x2.shape
    return pl.pallas_call(
        functools.partial(_ffn_ln_kernel, alpha=alpha),
        grid=(t // tm,),
        in_specs=[pl.BlockSpec((tm, d), lambda i: (i, 0)),
                  _resident(wg.shape), _resident(wu.shape), _resident(w2.shape),
                  _resident(g.shape), _resident(b.shape)],
        out_specs=pl.BlockSpec((tm, d), lambda i: (i, 0)),
        out_shape=jax.ShapeDtypeStruct((t, d), F32),
        compiler_params=_cparams(("parallel",)),
    )(x2, wg, wu, w2, g, b)


def _in_proj_kernel(x_ref, w_ref, ws_ref, cw_ref, hc_ref, kt_ref, sm_ref, raw_ref):
    tm = x_ref.shape[1]
    halo = SUBLANES
    per_chunk = PROJ_CHUNK // LANES
    n_conv = (COL_Z - COL_QB) // LANES

    @pl.when(pl.program_id(1) == 0)
    def _():
        raw_ref[:, 0:halo, :] = jnp.zeros((n_conv, halo, LANES), F32)

    xb = x_ref[0].astype(BF16)

    def proj(c):
        return jnp.dot(xb, w_ref[:, c * PROJ_CHUNK:(c + 1) * PROJ_CHUNK], preferred_element_type=F32)

    def conv_block(cb):
        acc = None
        for j in range(CONV_K):
            w = cw_ref[j:j + 1, cb * LANES:(cb + 1) * LANES]
            term = raw_ref[cb, halo - (CONV_K - 1) + j: halo - (CONV_K - 1) + j + tm, :] * w
            acc = term if acc is None else acc + term
        y = _silu(acc)
        kind = cb // N_HEADS_B
        if kind < 2:
            y = y * lax.rsqrt(jnp.sum(y * y, axis=-1, keepdims=True) + NORM_EPS)
        if kind == 0:
            y = y * (KEY_DIM_B ** -0.5)
        hc_ref[0, :, COL_QB + cb * LANES:COL_QB + (cb + 1) * LANES] = y.astype(BF16)
        if kind == 1:
            h = cb - N_HEADS_B
            kt_ref[0, h * LANES:(h + 1) * LANES, :] = y.T.astype(BF16)

    low = lax.broadcasted_iota(jnp.int32, (tm, LANES), 1) < HEAD_DIM_A

    def store_twice(r, col):
        for i in range(KV_A // LANES):
            two = r[:, i * LANES:(i + 1) * LANES]
            swapped = pltpu.roll(two, HEAD_DIM_A, 1)
            hc_ref[0, :, col + 2 * i * LANES:col + (2 * i + 1) * LANES] = jnp.where(low, two, swapped).astype(BF16)
            hc_ref[0, :, col + (2 * i + 1) * LANES:col + (2 * i + 2) * LANES] = jnp.where(low, swapped, two).astype(BF16)

    n_proj = W_MAIN // PROJ_CHUNK
    heavy = [c for c in range(n_proj) if W_QB <= c * PROJ_CHUNK < W_Z]
    light = [c for c in range(n_proj) if c not in heavy]
    order = []
    while heavy or light:
        if heavy:
            order.append(heavy.pop(0))
        if light:
            order.append(light.pop(0))
    for c in order:
        lo = c * PROJ_CHUNK
        r = proj(c)
        if lo < W_KV:
            hc_ref[0, :, COL_QA + lo:COL_QA + lo + PROJ_CHUNK] = r.astype(BF16)
        elif lo < W_QB:
            store_twice(r[:, :KV_A], COL_KA)
            store_twice(r[:, KV_A:], COL_VA)
        elif lo < W_Z:
            first = (lo - W_QB) // LANES
            for i in range(per_chunk):
                raw_ref[first + i, halo:halo + tm, :] = r[:, i * LANES:(i + 1) * LANES]
            for i in range(per_chunk):
                conv_block(first + i)
        elif lo < W_GA:
            hc_ref[0, :, COL_Z + lo - W_Z:COL_Z + lo - W_Z + PROJ_CHUNK] = _silu(r).astype(BF16)
        else:
            hc_ref[0, :, COL_GA + lo - W_GA:COL_GA + lo - W_GA + PROJ_CHUNK] = r.astype(BF16)
    small = jnp.dot(xb, ws_ref[...], preferred_element_type=F32)
    sm_ref[0] = small.T[:2 * N_HEADS_B, :]
    raw_ref[:, 0:halo, :] = raw_ref[:, tm:tm + halo, :]


def _in_proj(x3, w_main, w_small, conv_w, *, tm):
    b, l, d = x3.shape
    n_conv = (COL_Z - COL_QB) // LANES
    return pl.pallas_call(
        _in_proj_kernel,
        grid=(b, l // tm),
        in_specs=[pl.BlockSpec((1, tm, d), lambda i, j: (i, j, 0)),
                  _resident(w_main.shape), _resident(w_small.shape), _resident(conv_w.shape)],
        out_specs=[pl.BlockSpec((1, tm, N_MAIN), lambda i, j: (i, j, 0)),
                   pl.BlockSpec((1, QK_B, tm), lambda i, j: (i, 0, j)),
                   pl.BlockSpec((1, 2 * N_HEADS_B, tm), lambda i, j: (i, 0, j))],
        out_shape=[jax.ShapeDtypeStruct((b, l, N_MAIN), BF16),
                   jax.ShapeDtypeStruct((b, QK_B, l), BF16),
                   jax.ShapeDtypeStruct((b, 2 * N_HEADS_B, l), F32)],
        scratch_shapes=[pltpu.VMEM((n_conv, tm + SUBLANES, LANES), F32)],
        compiler_params=_cparams(("parallel", "arbitrary")),
    )(x3, w_main, w_small, conv_w)


def _t5_bucket(rel):
    n = jnp.maximum(rel, 0)
    max_exact = NUM_BUCKETS // 2
    nf = jnp.maximum(n, 1).astype(F32)
    large = max_exact + (jnp.log(nf / max_exact) / math.log(MAX_DISTANCE / max_exact)
                         * (NUM_BUCKETS - max_exact)).astype(jnp.int32)
    large = jnp.minimum(large, NUM_BUCKETS - 1)
    return jnp.where(n < max_exact, n, large)


def _pos_bias_kernel(rb_ref, bucket_ref, rel_ref, o_ref):
    h = pl.program_id(0)
    bucket = bucket_ref[...]
    rel = rel_ref[...]
    bias = jnp.zeros(bucket.shape, F32)
    for bk in range(NUM_BUCKETS):
        bias = jnp.where(bucket == bk, rb_ref[bk, h], bias)
    in_band = (rel >= 0) & (rel < WINDOW)
    own = lax.broadcasted_iota(jnp.int32, rel.shape, 1) >= WINDOW
    bias = bias * LOG2E
    o_ref[0, 0] = jnp.where(in_band & own, bias, NEG_INF)
    o_ref[1, 0] = jnp.where(in_band, bias, NEG_INF)


def _pos_bias(rel_bias):
    r = jnp.arange(WINDOW, dtype=jnp.int32)[:, None]
    j = jnp.arange(2 * WINDOW, dtype=jnp.int32)[None, :]
    rel = r + WINDOW - j
    bucket = _t5_bucket(rel).astype(jnp.int32)
    full = pl.BlockSpec((WINDOW, 2 * WINDOW), lambda h: (0, 0))
    return pl.pallas_call(
        _pos_bias_kernel,
        grid=(N_HEADS_A,),
        in_specs=[pl.BlockSpec(memory_space=pltpu.SMEM), full, full],
        out_specs=pl.BlockSpec((2, 1, WINDOW, 2 * WINDOW), lambda h: (0, h, 0, 0)),
        out_shape=jax.ShapeDtypeStruct((2, N_HEADS_A, WINDOW, 2 * WINDOW), F32),
        compiler_params=_cparams(("arbitrary",)),
    )(rel_bias, bucket, rel)


def _attn_kernel(sink_ref, q_ref, kc_ref, kp_ref, vc_ref, vp_ref, bias0_ref, bias1_ref, o_ref):
    grp = N_HEADS_A // N_KV_A
    items = [(qb, g) for qb in range(ATTN_BLOCKS) for g in range(N_KV_A)]
    low = lax.broadcasted_iota(jnp.int32, (WINDOW, LANES), 1) < HEAD_DIM_A
    zero = jnp.zeros((WINDOW, LANES), BF16)
    ones = jnp.ones((2 * WINDOW, LANES), BF16)
    nt = (((1,), (1,)), ((), ()))

    def q_stack(qb, g):
        rows = slice(qb * WINDOW, (qb + 1) * WINDOW)
        parts = []
        for pair in range(grp // 2):
            qp = q_ref[0, rows, pl.ds((g * grp + 2 * pair) * HEAD_DIM_A, LANES)]
            parts += [jnp.where(low, qp, zero), jnp.where(low, zero, qp)]
        return jnp.concatenate(parts, axis=0)

    def kv_cat(prev_ref, cur_ref, qb, g):
        cols = slice(g * LANES, (g + 1) * LANES)
        if qb == 0:
            return jnp.concatenate([prev_ref[0, :, cols], cur_ref[0, 0:WINDOW, cols]], axis=0)
        return cur_ref[0, (qb - 1) * WINDOW:(qb + 1) * WINDOW, cols]

    def bias(qb, g):
        ref = bias0_ref if qb == 0 else bias1_ref
        return ref[0, g * grp:(g + 1) * grp].reshape(grp * WINDOW, 2 * WINDOW)

    sinks = [jnp.concatenate([jnp.full((WINDOW, LANES), sink_ref[g * grp + h] * LOG2E, F32) for h in range(grp)], axis=0)
             for g in range(N_KV_A)]
    q4s = [q_stack(qb, g) for qb, g in items]
    kks = [kv_cat(kp_ref, kc_ref, qb, g) for qb, g in items]
    vxs = [jnp.concatenate([kv_cat(vp_ref, vc_ref, qb, g), ones], axis=1) for qb, g in items]
    ss = [lax.dot_general(q4, kk, nt, preferred_element_type=F32) + bias(qb, g)
          for (qb, g), q4, kk in zip(items, q4s, kks)]
    ms = [jnp.maximum(jnp.broadcast_to(jnp.max(jnp.maximum(s[:, :LANES], s[:, LANES:]), axis=-1, keepdims=True),
                                       (grp * WINDOW, LANES)), sinks[g]) for (_, g), s in zip(items, ss)]
    ps = [jnp.exp2(s - jnp.concatenate([m, m], axis=1)).astype(BF16) for s, m in zip(ss, ms)]
    ols = [jnp.dot(p, vx, preferred_element_type=F32) for p, vx in zip(ps, vxs)]
    for (qb, g), ol, m in zip(items, ols, ms):
        o = ol[:, :LANES] / (ol[:, LANES:] + jnp.exp2(sinks[g] - m))
        for pair in range(grp // 2):
            r0 = (2 * pair) * WINDOW
            both = jnp.where(low, o[r0:r0 + WINDOW], o[r0 + WINDOW:r0 + 2 * WINDOW])
            o_ref[0, qb * WINDOW:(qb + 1) * WINDOW, pl.ds((g * grp + 2 * pair) * HEAD_DIM_A, LANES)] = both.astype(BF16)


def _attention(hc, pos_bias, sink):
    b, l, _ = hc.shape
    rows = ATTN_BLOCKS * WINDOW
    kblk, vblk = COL_KA // KDUP, COL_VA // KDUP
    prev = lambda i, j: (i, jnp.maximum(ATTN_BLOCKS * j - 1, 0))
    bias_spec = lambda f: pl.BlockSpec((1, N_HEADS_A, WINDOW, 2 * WINDOW), f)
    return pl.pallas_call(
        _attn_kernel,
        grid=(b, l // rows),
        in_specs=[pl.BlockSpec(memory_space=pltpu.SMEM),
                  pl.BlockSpec((1, rows, Q_A), lambda i, j: (i, j, COL_QA // Q_A)),
                  pl.BlockSpec((1, rows, KDUP), lambda i, j: (i, j, kblk)),
                  pl.BlockSpec((1, WINDOW, KDUP), lambda i, j: (*prev(i, j), kblk)),
                  pl.BlockSpec((1, rows, KDUP), lambda i, j: (i, j, vblk)),
                  pl.BlockSpec((1, WINDOW, KDUP), lambda i, j: (*prev(i, j), vblk)),
                  bias_spec(lambda i, j: (jnp.minimum(j, 1), 0, 0, 0)),
                  bias_spec(lambda i, j: (1, 0, 0, 0))],
        out_specs=pl.BlockSpec((1, rows, Q_A), lambda i, j: (i, j, 0)),
        out_shape=jax.ShapeDtypeStruct((b, l, Q_A), BF16),
        compiler_params=_cparams(("parallel", "arbitrary")),
    )(sink, hc, hc, hc, hc, hc, pos_bias, pos_bias)


def _chunk_sums(g):
    c = DELTA_CHUNK
    n = g.shape[1] // c
    hi = g.astype(BF16).astype(F32)
    r1 = g - hi
    mid = r1.astype(BF16).astype(F32)
    lo = r1 - mid
    s = lax.broadcasted_iota(jnp.int32, (c, 2 * c), 0)
    t = lax.broadcasted_iota(jnp.int32, (c, 2 * c), 1)
    tri = jnp.where(((t < c) & (s <= t)) | ((t >= c) & (s > t - c)), 1.0, 0.0).astype(BF16)
    lhs = jnp.concatenate([piece[:, ci * c:(ci + 1) * c] for ci in range(n) for piece in (hi, mid, lo)],
                          axis=0).astype(BF16)
    out = jnp.dot(lhs, tri, preferred_element_type=F32)
    k = g.shape[0]
    both = [out[3 * k * ci:3 * k * ci + k] + out[3 * k * ci + k:3 * k * ci + 2 * k]
            + out[3 * k * ci + 2 * k:3 * k * (ci + 1)] for ci in range(n)]
    prefix = jnp.concatenate([bt[:, :c] for bt in both], axis=1)
    suffix = jnp.concatenate([bt[:, c:] for bt in both], axis=1)
    return prefix, suffix


def _halves(t):
    return t[:, :LANES], t[:, LANES:]


def _pair(a, b):
    return jnp.concatenate([a, b], axis=1)


def _block_diag(t):
    a, b = _halves(t)
    z = jnp.zeros_like(a)
    return jnp.concatenate([_pair(a, z), _pair(z, b)], axis=0)


def _mm2(a, b):
    return jnp.dot(a.astype(BF16), _block_diag(b.astype(BF16)), preferred_element_type=F32)


def _scatter_rows(t, stripes, n):
    parts, at, used = [], 0, 0
    for s in stripes:
        if s.start > at:
            parts.append(jnp.zeros((s.start - at, t.shape[1]), t.dtype))
        size = s.stop - s.start
        parts.append(t[used:used + size])
        used += size
        at = s.stop
    if at < n:
        parts.append(jnp.zeros((n - at, t.shape[1]), t.dtype))
    return jnp.concatenate(parts, axis=0)


def _unit_lower_inverses(lms, row, col):
    c = row.shape[0]
    eye = (row == col).astype(F32)
    base = SUBLANES
    same = lambda size: (row & -size) == (col & -size)
    lds = [jnp.where(same(base), lm, 0.0) for lm in lms]
    a1s = [eye - ld for ld in lds]
    ldbs = [ld.astype(BF16) for ld in lds]
    ld2bs = [_mm2(ldb, ldb).astype(BF16) for ldb in ldbs]
    ld4s = [_mm2(ld2b, ld2b) for ld2b in ld2bs]
    p1s = [a1 + _mm2(a1, ld2b) for a1, ld2b in zip(a1s, ld2bs)]
    xs = [p1 + _mm2(p1, ld4) for p1, ld4 in zip(p1s, ld4s)]
    lmbs = [lm.astype(BF16) for lm in lms]
    zero = jnp.zeros(row.shape, BF16)
    blk = base
    while blk < c:
        mask = same(2 * blk) & jnp.logical_not(same(blk))
        xbs = [x.astype(BF16) for x in xs]
        if blk % (2 * SUBLANES) == 0:
            stripes = [slice(r0, r0 + blk) for r0 in range(blk, c, 2 * blk)]
            rows_of = lambda t: jnp.concatenate([t[s] for s in stripes], axis=0)
            exs = [_mm2(rows_of(jnp.where(mask, lmb, zero)), xb) for lmb, xb in zip(lmbs, xbs)]
            upd = [_mm2(rows_of(xb), _scatter_rows(ex, stripes, c)) for xb, ex in zip(xbs, exs)]
            xs = [x - _scatter_rows(u, stripes, c) for x, u in zip(xs, upd)]
        else:
            exs = [_mm2(jnp.where(mask, lmb, zero), xb) for lmb, xb in zip(lmbs, xbs)]
            xs = [x - _mm2(xb, ex) for x, xb, ex in zip(xs, xbs, exs)]
        blk *= 2
    return xs


def _delta_kernel(q_ref, k_ref, v_ref, kt_ref, z_ref, sm_ref, alog_ref, dtb_ref, ng_ref, o_ref, s_ref, *, n_chunks):
    c = DELTA_CHUNK
    blk = n_chunks * c

    @pl.when(pl.program_id(1) == 0)
    def _():
        s_ref[...] = jnp.zeros(s_ref.shape, F32)

    sm = sm_ref[0]
    beta = jax.nn.sigmoid(sm[0:N_HEADS_B])
    log_beta = -jax.nn.softplus(-sm[0:N_HEADS_B])
    glog = -jnp.exp(alog_ref[...]) * jax.nn.softplus(sm[N_HEADS_B:2 * N_HEADS_B] + dtb_ref[...])
    gc, tail = _chunk_sums(glog)
    etail = jnp.exp(tail)
    egc = jnp.exp(gc)
    beta_egc = beta * egc
    rows = jnp.concatenate([gc + log_beta, gc, egc, jnp.exp(gc + tail),
                            jnp.zeros((LANES - 4 * N_HEADS_B, blk), F32)], axis=0)
    cols = rows.T

    n_pairs = N_HEADS_B // 2
    wide = 2 * LANES
    row_i = lax.broadcasted_iota(jnp.int32, (c, wide), 0)
    col_i = lax.broadcasted_iota(jnp.int32, (c, wide), 1) & (LANES - 1)
    causal = row_i >= col_i
    strict = row_i > col_i

    def col2(kind, p, rows_):
        base = kind * N_HEADS_B + 2 * p
        return _pair(jnp.broadcast_to(cols[rows_, base:base + 1], (rows_.stop - rows_.start, LANES)),
                     jnp.broadcast_to(cols[rows_, base + 1:base + 2], (rows_.stop - rows_.start, LANES)))

    def row2(arr, p, t):
        return _pair(arr[2 * p:2 * p + 1, t], arr[2 * p + 1:2 * p + 2, t])

    lanes_of = lambda p: slice(p * wide, (p + 1) * wide)

    def state_free_part(chunk_ids):
        items = [(p, ci) for ci in chunk_ids for p in range(n_pairs)]
        pt = [(p, slice(ci * c, (ci + 1) * c)) for p, ci in items]
        qs = [q_ref[0, t, lanes_of(p)] for p, t in pt]
        ks = [k_ref[0, t, lanes_of(p)] for p, t in pt]
        vs = [v_ref[0, t, lanes_of(p)] for p, t in pt]
        kts = [_pair(kt_ref[0, 2 * p * LANES:(2 * p + 1) * LANES, t],
                     kt_ref[0, (2 * p + 1) * LANES:(2 * p + 2) * LANES, t]) for p, t in pt]
        gcbs = [col2(0, p, t) for p, t in pt]
        gccs = [col2(1, p, t) for p, t in pt]
        egcs = [col2(2, p, t) for p, t in pt]
        gcrs = [row2(gc, p, t) for p, t in pt]
        etrs = [row2(etail, p, t) for p, t in pt]
        brs = [row2(beta, p, t) for p, t in pt]
        bers = [row2(beta_egc, p, t) for p, t in pt]

        prods = [jnp.dot(jnp.concatenate([k, q], axis=0), _block_diag(kt), preferred_element_type=F32)
                 for k, q, kt in zip(ks, qs, kts)]
        lms = [pr[:c] * jnp.exp(jnp.where(strict, gcb - gcr, NEG_INF)) for pr, gcb, gcr in zip(prods, gcbs, gcrs)]
        a_intras = [pr[c:] * jnp.exp(jnp.where(causal, gcc - gcr, NEG_INF)) for pr, gcc, gcr in zip(prods, gccs, gcrs)]
        t_invs = _unit_lower_inverses(lms, row_i, col_i)
        us = [jnp.dot((t_inv * br).astype(BF16), _block_diag(v), preferred_element_type=F32)
              for t_inv, br, v in zip(t_invs, brs, vs)]
        ws = [jnp.dot((t_inv * ber).astype(BF16), _block_diag(k), preferred_element_type=F32)
              for t_inv, ber, k in zip(t_invs, bers, ks)]
        wqs = [jnp.concatenate([w.astype(BF16), (q * egc).astype(BF16)], axis=0)
               for w, q, egc in zip(ws, qs, egcs)]
        akts = [jnp.concatenate([a.astype(BF16), (kt * etr).astype(BF16)], axis=0)
                for a, kt, etr in zip(a_intras, kts, etrs)]

        return us, wqs, akts

    us, wqs, akts = [], [], []
    for first in range(0, n_chunks, DELTA_GROUP):
        part = state_free_part(range(first, min(first + DELTA_GROUP, n_chunks)))
        us += part[0]
        wqs += part[1]
        akts += part[2]

    states = [s_ref[p] for p in range(n_pairs)]
    for ci in range(n_chunks):
        idx = [ci * n_pairs + p for p in range(n_pairs)]
        tok = slice(ci * c, (ci + 1) * c)
        r1 = [jnp.dot(wqs[i], _block_diag(states[p].astype(BF16)), preferred_element_type=F32)
              for p, i in enumerate(idx)]
        v_news = [us[i] - r[:c] for i, r in zip(idx, r1)]
        r2 = [jnp.dot(akts[i], _block_diag(vn.astype(BF16)), preferred_element_type=F32)
              for i, vn in zip(idx, v_news)]
        for p in range(n_pairs):
            base = 3 * N_HEADS_B + 2 * p
            egl = _pair(jnp.broadcast_to(cols[ci * c:ci * c + 1, base:base + 1], (1, LANES)),
                        jnp.broadcast_to(cols[ci * c:ci * c + 1, base + 1:base + 2], (1, LANES)))
            states[p] = states[p] * egl + r2[p][c:]
            o = r1[p][c:] + r2[p][:c]
            o = _pair(*[u * lax.rsqrt(jnp.mean(u * u, axis=-1, keepdims=True) + NORM_EPS) * ng_ref[...]
                        for u in _halves(o)])
            o_ref[0, tok, lanes_of(p)] = (o * z_ref[0, tok, lanes_of(p)]).astype(BF16)
    for p in range(n_pairs):
        s_ref[p] = states[p]


def _delta(hc, kt, small, a_log, dt_bias, norm_g, *, n_chunks):
    b, l, _ = hc.shape
    blk = n_chunks * DELTA_CHUNK
    col = lambda cidx: pl.BlockSpec((1, blk, QK_B), lambda i, j: (i, j, cidx))
    return pl.pallas_call(
        functools.partial(_delta_kernel, n_chunks=n_chunks),
        grid=(b, l // blk),
        in_specs=[col(COL_QB // QK_B), col(COL_KB // QK_B), col(COL_VB // QK_B),
                  pl.BlockSpec((1, QK_B, blk), lambda i, j: (i, 0, j)),
                  col(COL_Z // QK_B),
                  pl.BlockSpec((1, 2 * N_HEADS_B, blk), lambda i, j: (i, 0, j)),
                  _resident(a_log.shape), _resident(dt_bias.shape), _resident(norm_g.shape)],
        out_specs=pl.BlockSpec((1, blk, V_B), lambda i, j: (i, j, 0)),
        out_shape=jax.ShapeDtypeStruct((b, l, V_B), BF16),
        scratch_shapes=[pltpu.VMEM((N_HEADS_B // 2, KEY_DIM_B, 2 * VAL_DIM_B), F32)],
        compiler_params=_cparams(("parallel", "arbitrary")),
    )(hc, hc, hc, kt, hc, small, a_log, dt_bias, norm_g)


def _merge_kernel(x_ref, ya_ref, yb_ref, ga_ref, gb_ref, wa_ref, wb_ref, wo_ref, g_ref, b_ref, o_ref, *, alpha):
    half = x_ref.shape[0] // 2
    for part in range(2):
        rows = slice(part * half, (part + 1) * half)
        ya = jnp.dot(ya_ref[rows, :], wa_ref[...], preferred_element_type=F32)
        yb = jnp.dot(yb_ref[rows, :], wb_ref[...], preferred_element_type=F32)
        mix = (jax.nn.sigmoid(ga_ref[rows, :].astype(F32)) * ya
               + jax.nn.sigmoid(gb_ref[rows, :].astype(F32)) * yb)
        y = jnp.dot(mix.astype(BF16), wo_ref[...], preferred_element_type=F32)
        o_ref[rows, :] = _layernorm(alpha * x_ref[rows, :] + y, g_ref[...], b_ref[...])


def _merge(x2, ya, yb, hc2, wa, wb, wo, g, b, *, alpha, tm):
    t, d = x2.shape
    tile = lambda cidx=0: pl.BlockSpec((tm, d), lambda i: (i, cidx))
    return pl.pallas_call(
        functools.partial(_merge_kernel, alpha=alpha),
        grid=(t // tm,),
        in_specs=[tile(), tile(), tile(), tile(COL_GA // Q_A), tile(COL_GA // Q_A + 1),
                  _resident(wa.shape), _resident(wb.shape), _resident(wo.shape),
                  _resident(g.shape), _resident(b.shape)],
        out_specs=tile(),
        out_shape=jax.ShapeDtypeStruct((t, d), F32),
        compiler_params=_cparams(("parallel",)),
    )(x2, ya, yb, hc2, hc2, wa, wb, wo, g, b)


def _prep_w_in(w_in):
    d = w_in.shape[0]
    narrow = W_Z
    w_main = jnp.concatenate([w_in[:, :Q_A] * (HEAD_DIM_A ** -0.5 * LOG2E), w_in[:, Q_A:narrow],
                              w_in[:, narrow + 2 * N_HEADS_B:]], axis=1).astype(BF16)
    w_small = jnp.concatenate([w_in[:, narrow:narrow + 2 * N_HEADS_B],
                               jnp.zeros((d, LANES - 2 * N_HEADS_B), w_in.dtype)], axis=1).astype(BF16)
    return w_main, w_small


def _prep_ffn(w13, w2):
    f = w13.shape[1] // 2
    return w13[:, :f].astype(BF16), w13[:, f:].astype(BF16), w2.astype(BF16)


def kernel(x, rel_bias, ln_g, ln_b, ffn_w13, ffn_w2, w_in, conv_w, a_log, dt_bias, dn_norm_g, sinks,
           w_branch_a, w_branch_b, w_out):
    b, l, d = x.shape
    depth = w_in.shape[0]
    alpha = (2 * depth) ** 0.25
    t = b * l
    tm = 512
    pos_bias = _pos_bias(rel_bias)
    x2 = x.reshape(t, d)
    row = lambda v: v.reshape(1, -1)
    colv = lambda v: v.reshape(-1, 1)
    for i in range(depth):
        x2 = _ffn_ln(x2, *_prep_ffn(ffn_w13[i, 0], ffn_w2[i, 0]), row(ln_g[i, 0]), row(ln_b[i, 0]),
                     alpha=alpha, tm=2 * tm)
        w_main, w_small = _prep_w_in(w_in[i])
        hc, kt, small = _in_proj(x2.reshape(b, l, d), w_main, w_small, conv_w[i], tm=tm)
        ya = _attention(hc, pos_bias, sinks[i])
        yb = _delta(hc, kt, small, colv(a_log[i]), colv(dt_bias[i]), row(dn_norm_g[i]), n_chunks=8)
        x2 = _merge(x2, ya.reshape(t, Q_A), yb.reshape(t, V_B), hc.reshape(t, N_MAIN),
                    w_branch_a[i].astype(BF16), w_branch_b[i].astype(BF16), w_out[i].astype(BF16),
                    row(ln_g[i, 1]), row(ln_b[i, 1]), alpha=alpha, tm=2 * tm)
        x2 = _ffn_ln(x2, *_prep_ffn(ffn_w13[i, 1], ffn_w2[i, 1]), row(ln_g[i, 2]), row(ln_b[i, 2]),
                     alpha=alpha, tm=2 * tm)
    return x2.reshape(b, l, d)
```

```python
import functools
import math

import jax
import jax.numpy as jnp
from jax import lax
from jax.experimental import pallas as pl
from jax.experimental.pallas import tpu as pltpu

BF16 = jnp.bfloat16
F32 = jnp.float32

N_HEADS_A = 16
N_KV_A = 4
HEAD_DIM_A = 64
WINDOW = 128
N_HEADS_B = 8
KEY_DIM_B = 128
VAL_DIM_B = 128
CONV_K = 4
NUM_BUCKETS = 32
MAX_DISTANCE = 128
LOG2E = 1.4426950408889634
LN_EPS = 1e-5
NORM_EPS = 1e-6
NEG_INF = -1e30

LANES = 128
SUBLANES = 8
VMEM_LIMIT_BYTES = 58 * 1024 * 1024

Q_A = N_HEADS_A * HEAD_DIM_A
KDUP = N_KV_A * 2 * HEAD_DIM_A
QK_B = N_HEADS_B * KEY_DIM_B
V_B = N_HEADS_B * VAL_DIM_B
COL_QA = 0
COL_KA = COL_QA + Q_A
COL_VA = COL_KA + KDUP
COL_QB = COL_VA + KDUP
COL_KB = COL_QB + QK_B
COL_VB = COL_KB + QK_B
COL_Z = COL_VB + V_B
COL_GA = COL_Z + V_B
N_MAIN = COL_GA + 2 * Q_A

KV_A = N_KV_A * HEAD_DIM_A
W_KV = Q_A
W_QB = W_KV + 2 * KV_A
W_Z = W_QB + 2 * QK_B + V_B
W_GA = W_Z + V_B
W_MAIN = W_GA + 2 * Q_A

DELTA_CHUNK = 128
FFN_CHUNK = 256
PROJ_CHUNK = 512
DELTA_GROUP = 2
ATTN_BLOCKS = 8


def _cparams(sem):
    return pltpu.CompilerParams(dimension_semantics=sem, vmem_limit_bytes=VMEM_LIMIT_BYTES)


def _resident(shape):
    nd = len(shape)
    return pl.BlockSpec(shape, lambda *_: (0,) * nd, pipeline_mode=pl.Buffered(1))


def _layernorm(y, g, b):
    mu = jnp.mean(y, axis=-1, keepdims=True)
    d = y - mu
    var = jnp.mean(d * d, axis=-1, keepdims=True)
    return d * lax.rsqrt(var + LN_EPS) * g + b


def _silu(x):
    return x * jax.nn.sigmoid(x)


def _ffn_ln_kernel(x_ref, wg_ref, wu_ref, w2_ref, g_ref, b_ref, o_ref, *, alpha):
    half = x_ref.shape[0] // 2
    for part in range(2):
        rows = slice(part * half, (part + 1) * half)
        x = x_ref[rows, :]
        xb = x.astype(BF16)
        acc = jnp.zeros(x.shape, F32)
        for c in range(wg_ref.shape[1] // FFN_CHUNK):
            cols = slice(c * FFN_CHUNK, (c + 1) * FFN_CHUNK)
            h = jnp.dot(xb, wg_ref[:, cols], preferred_element_type=F32)
            u = jnp.dot(xb, wu_ref[:, cols], preferred_element_type=F32)
            a = (_silu(h) * u).astype(BF16)
            acc = acc + jnp.dot(a, w2_ref[cols, :], preferred_element_type=F32)
        o_ref[rows, :] = _layernorm(alpha * x + 0.5 * acc, g_ref[...], b_ref[...])


def _ffn_ln(x2, wg, wu, w2, g, b, *, alpha, tm):
    t, d = x2.shape
    return pl.pallas_call(
        functools.partial(_ffn_ln_kernel, alpha=alpha),
        grid=(t // tm,),
        in_specs=[pl.BlockSpec((tm, d), lambda i: (i, 0)),
                  _resident(wg.shape), _resident(wu.shape), _resident(w2.shape),
                  _resident(g.shape), _resident(b.shape)],
        out_specs=pl.BlockSpec((tm, d), lambda i: (i, 0)),
        out_shape=jax.ShapeDtypeStruct((t, d), F32),
        compiler_params=_cparams(("parallel",)),
    )(x2, wg, wu, w2, g, b)


def _in_proj_kernel(x_ref, w_ref, ws_ref, cw_ref, hc_ref, kt_ref, sm_ref, raw_ref):
    tm = x_ref.shape[1]
    halo = SUBLANES
    per_chunk = PROJ_CHUNK // LANES
    n_conv = (COL_Z - COL_QB) // LANES

    @pl.when(pl.program_id(1) == 0)
    def _():
        raw_ref[:, :, 0:2 * halo, :] = jnp.zeros((CONV_K - 1, n_conv, 2 * halo, LANES), F32)

    xb = x_ref[0].astype(BF16)

    def proj(c):
        return jnp.dot(xb, w_ref[:, c * PROJ_CHUNK:(c + 1) * PROJ_CHUNK], preferred_element_type=F32)

    def conv_block(cb, cur):
        acc = cur * cw_ref[CONV_K - 1:CONV_K, cb * LANES:(cb + 1) * LANES]
        for j in range(CONV_K - 1):
            w = cw_ref[j:j + 1, cb * LANES:(cb + 1) * LANES]
            acc = acc + raw_ref[j, cb, halo:halo + tm, :] * w
        y = _silu(acc)
        kind = cb // N_HEADS_B
        if kind < 2:
            y = y * lax.rsqrt(jnp.sum(y * y, axis=-1, keepdims=True) + NORM_EPS)
        if kind == 0:
            y = y * (KEY_DIM_B ** -0.5)
        hc_ref[0, :, COL_QB + cb * LANES:COL_QB + (cb + 1) * LANES] = y.astype(BF16)
        if kind == 1:
            h = cb - N_HEADS_B
            kt_ref[0, h * LANES:(h + 1) * LANES, :] = y.T.astype(BF16)

    low = lax.broadcasted_iota(jnp.int32, (tm, LANES), 1) < HEAD_DIM_A

    def store_twice(r, col):
        for i in range(KV_A // LANES):
            two = r[:, i * LANES:(i + 1) * LANES]
            swapped = pltpu.roll(two, HEAD_DIM_A, 1)
            hc_ref[0, :, col + 2 * i * LANES:col + (2 * i + 1) * LANES] = jnp.where(low, two, swapped).astype(BF16)
            hc_ref[0, :, col + (2 * i + 1) * LANES:col + (2 * i + 2) * LANES] = jnp.where(low, swapped, two).astype(BF16)

    n_proj = W_MAIN // PROJ_CHUNK
    heavy = [c for c in range(n_proj) if W_QB <= c * PROJ_CHUNK < W_Z]
    light = [c for c in range(n_proj) if c not in heavy]
    order = []
    while heavy or light:
        if heavy:
            order.append(heavy.pop(0))
        if light:
            order.append(light.pop(0))
    for c in order:
        lo = c * PROJ_CHUNK
        r = proj(c)
        if lo < W_KV:
            hc_ref[0, :, COL_QA + lo:COL_QA + lo + PROJ_CHUNK] = r.astype(BF16)
        elif lo < W_QB:
            store_twice(r[:, :KV_A], COL_KA)
            store_twice(r[:, KV_A:], COL_VA)
        elif lo < W_Z:
            first = (lo - W_QB) // LANES
            for i in range(per_chunk):
                for j in range(CONV_K - 1):
                    d = CONV_K - 1 - j
                    raw_ref[j, first + i, halo + d:halo + d + tm, :] = r[:, i * LANES:(i + 1) * LANES]
            for i in range(per_chunk):
                conv_block(first + i, r[:, i * LANES:(i + 1) * LANES])
        elif lo < W_GA:
            hc_ref[0, :, COL_Z + lo - W_Z:COL_Z + lo - W_Z + PROJ_CHUNK] = _silu(r).astype(BF16)
        else:
            hc_ref[0, :, COL_GA + lo - W_GA:COL_GA + lo - W_GA + PROJ_CHUNK] = r.astype(BF16)
    small = jnp.dot(xb, ws_ref[...], preferred_element_type=F32)
    sm_ref[0] = small.T[:2 * N_HEADS_B, :]
    raw_ref[:, :, halo:2 * halo, :] = raw_ref[:, :, tm + halo:tm + 2 * halo, :]


def _in_proj(x3, w_main, w_small, conv_w, *, tm):
    b, l, d = x3.shape
    n_conv = (COL_Z - COL_QB) // LANES
    return pl.pallas_call(
        _in_proj_kernel,
        grid=(b, l // tm),
        in_specs=[pl.BlockSpec((1, tm, d), lambda i, j: (i, j, 0)),
                  _resident(w_main.shape), _resident(w_small.shape), _resident(conv_w.shape)],
        out_specs=[pl.BlockSpec((1, tm, N_MAIN), lambda i, j: (i, j, 0)),
                   pl.BlockSpec((1, QK_B, tm), lambda i, j: (i, 0, j)),
                   pl.BlockSpec((1, 2 * N_HEADS_B, tm), lambda i, j: (i, 0, j))],
        out_shape=[jax.ShapeDtypeStruct((b, l, N_MAIN), BF16),
                   jax.ShapeDtypeStruct((b, QK_B, l), BF16),
                   jax.ShapeDtypeStruct((b, 2 * N_HEADS_B, l), F32)],
        scratch_shapes=[pltpu.VMEM((CONV_K - 1, n_conv, tm + 2 * SUBLANES, LANES), F32)],
        compiler_params=_cparams(("parallel", "arbitrary")),
    )(x3, w_main, w_small, conv_w)


def _t5_bucket(rel):
    n = jnp.maximum(rel, 0)
    max_exact = NUM_BUCKETS // 2
    nf = jnp.maximum(n, 1).astype(F32)
    large = max_exact + (jnp.log(nf / max_exact) / math.log(MAX_DISTANCE / max_exact)
                         * (NUM_BUCKETS - max_exact)).astype(jnp.int32)
    large = jnp.minimum(large, NUM_BUCKETS - 1)
    return jnp.where(n < max_exact, n, large)


def _pos_bias_kernel(rb_ref, bucket_ref, rel_ref, o_ref):
    h = pl.program_id(0)
    bucket = bucket_ref[...]
    rel = rel_ref[...]
    bias = jnp.zeros(bucket.shape, F32)
    for bk in range(NUM_BUCKETS):
        bias = jnp.where(bucket == bk, rb_ref[bk, h], bias)
    in_band = (rel >= 0) & (rel < WINDOW)
    own = lax.broadcasted_iota(jnp.int32, rel.shape, 1) >= WINDOW
    bias = bias * LOG2E
    o_ref[0, 0] = jnp.where(in_band & own, bias, NEG_INF)
    o_ref[1, 0] = jnp.where(in_band, bias, NEG_INF)


def _pos_bias(rel_bias):
    r = jnp.arange(WINDOW, dtype=jnp.int32)[:, None]
    j = jnp.arange(2 * WINDOW, dtype=jnp.int32)[None, :]
    rel = r + WINDOW - j
    bucket = _t5_bucket(rel).astype(jnp.int32)
    full = pl.BlockSpec((WINDOW, 2 * WINDOW), lambda h: (0, 0))
    return pl.pallas_call(
        _pos_bias_kernel,
        grid=(N_HEADS_A,),
        in_specs=[pl.BlockSpec(memory_space=pltpu.SMEM), full, full],
        out_specs=pl.BlockSpec((2, 1, WINDOW, 2 * WINDOW), lambda h: (0, h, 0, 0)),
        out_shape=jax.ShapeDtypeStruct((2, N_HEADS_A, WINDOW, 2 * WINDOW), F32),
        compiler_params=_cparams(("arbitrary",)),
    )(rel_bias, bucket, rel)


def _attn_kernel(sink_ref, q_ref, kc_ref, kp_ref, vc_ref, vp_ref, bias0_ref, bias1_ref, o_ref):
    grp = N_HEADS_A // N_KV_A
    items = [(qb, g) for qb in range(ATTN_BLOCKS) for g in range(N_KV_A)]
    low = lax.broadcasted_iota(jnp.int32, (WINDOW, LANES), 1) < HEAD_DIM_A
    zero = jnp.zeros((WINDOW, LANES), BF16)
    ones = jnp.ones((2 * WINDOW, LANES), BF16)
    nt = (((1,), (1,)), ((), ()))

    def q_stack(qb, g):
        rows = slice(qb * WINDOW, (qb + 1) * WINDOW)
        parts = []
        for pair in range(grp // 2):
            qp = q_ref[0, rows, pl.ds((g * grp + 2 * pair) * HEAD_DIM_A, LANES)]
            parts += [jnp.where(low, qp, zero), jnp.where(low, zero, qp)]
        return jnp.concatenate(parts, axis=0)

    def kv_cat(prev_ref, cur_ref, qb, g):
        cols = slice(g * LANES, (g + 1) * LANES)
        if qb == 0:
            return jnp.concatenate([prev_ref[0, :, cols], cur_ref[0, 0:WINDOW, cols]], axis=0)
        return cur_ref[0, (qb - 1) * WINDOW:(qb + 1) * WINDOW, cols]

    def bias(qb, g):
        ref = bias0_ref if qb == 0 else bias1_ref
        return ref[0, g * grp:(g + 1) * grp].reshape(grp * WINDOW, 2 * WINDOW)

    sinks = [jnp.concatenate([jnp.full((WINDOW, LANES), sink_ref[g * grp + h] * LOG2E, F32) for h in range(grp)], axis=0)
             for g in range(N_KV_A)]
    q4s = [q_stack(qb, g) for qb, g in items]
    kks = [kv_cat(kp_ref, kc_ref, qb, g) for qb, g in items]
    vxs = [jnp.concatenate([kv_cat(vp_ref, vc_ref, qb, g), ones], axis=1) for qb, g in items]
    ss = [lax.dot_general(q4, kk, nt, preferred_element_type=F32) + bias(qb, g)
          for (qb, g), q4, kk in zip(items, q4s, kks)]
    ms = [jnp.maximum(jnp.broadcast_to(jnp.max(jnp.maximum(s[:, :LANES], s[:, LANES:]), axis=-1, keepdims=True),
                                       (grp * WINDOW, LANES)), sinks[g]) for (_, g), s in zip(items, ss)]
    ps = [jnp.exp2(s - jnp.concatenate([m, m], axis=1)).astype(BF16) for s, m in zip(ss, ms)]
    ols = [jnp.dot(p, vx, preferred_element_type=F32) for p, vx in zip(ps, vxs)]
    for (qb, g), ol, m in zip(items, ols, ms):
        o = ol[:, :LANES] / (ol[:, LANES:] + jnp.exp2(sinks[g] - m))
        for pair in range(grp // 2):
            r0 = (2 * pair) * WINDOW
            both = jnp.where(low, o[r0:r0 + WINDOW], o[r0 + WINDOW:r0 + 2 * WINDOW])
            o_ref[0, qb * WINDOW:(qb + 1) * WINDOW, pl.ds((g * grp + 2 * pair) * HEAD_DIM_A, LANES)] = both.astype(BF16)


def _attention(hc, pos_bias, sink):
    b, l, _ = hc.shape
    rows = ATTN_BLOCKS * WINDOW
    kblk, vblk = COL_KA // KDUP, COL_VA // KDUP
    prev = lambda i, j: (i, jnp.maximum(ATTN_BLOCKS * j - 1, 0))
    bias_spec = lambda f: pl.BlockSpec((1, N_HEADS_A, WINDOW, 2 * WINDOW), f)
    return pl.pallas_call(
        _attn_kernel,
        grid=(b, l // rows),
        in_specs=[pl.BlockSpec(memory_space=pltpu.SMEM),
                  pl.BlockSpec((1, rows, Q_A), lambda i, j: (i, j, COL_QA // Q_A)),
                  pl.BlockSpec((1, rows, KDUP), lambda i, j: (i, j, kblk)),
                  pl.BlockSpec((1, WINDOW, KDUP), lambda i, j: (*prev(i, j), kblk)),
                  pl.BlockSpec((1, rows, KDUP), lambda i, j: (i, j, vblk)),
                  pl.BlockSpec((1, WINDOW, KDUP), lambda i, j: (*prev(i, j), vblk)),
                  bias_spec(lambda i, j: (jnp.minimum(j, 1), 0, 0, 0)),
                  bias_spec(lambda i, j: (1, 0, 0, 0))],
        out_specs=pl.BlockSpec((1, rows, Q_A), lambda i, j: (i, j, 0)),
        out_shape=jax.ShapeDtypeStruct((b, l, Q_A), BF16),
        compiler_params=_cparams(("parallel", "arbitrary")),
    )(sink, hc, hc, hc, hc, hc, pos_bias, pos_bias)


def _chunk_sums(g):
    c = DELTA_CHUNK
    n = g.shape[1] // c
    hi = g.astype(BF16).astype(F32)
    r1 = g - hi
    mid = r1.astype(BF16).astype(F32)
    lo = r1 - mid
    s = lax.broadcasted_iota(jnp.int32, (c, 2 * c), 0)
    t = lax.broadcasted_iota(jnp.int32, (c, 2 * c), 1)
    tri = jnp.where(((t < c) & (s <= t)) | ((t >= c) & (s > t - c)), 1.0, 0.0).astype(BF16)
    lhs = jnp.concatenate([piece[:, ci * c:(ci + 1) * c] for ci in range(n) for piece in (hi, mid, lo)],
                          axis=0).astype(BF16)
    out = jnp.dot(lhs, tri, preferred_element_type=F32)
    k = g.shape[0]
    both = [out[3 * k * ci:3 * k * ci + k] + out[3 * k * ci + k:3 * k * ci + 2 * k]
            + out[3 * k * ci + 2 * k:3 * k * (ci + 1)] for ci in range(n)]
    prefix = jnp.concatenate([bt[:, :c] for bt in both], axis=1)
    suffix = jnp.concatenate([bt[:, c:] for bt in both], axis=1)
    return prefix, suffix


def _halves(t):
    return t[:, :LANES], t[:, LANES:]


def _pair(a, b):
    return jnp.concatenate([a, b], axis=1)


def _block_diag(t):
    a, b = _halves(t)
    z = jnp.zeros_like(a)
    return jnp.concatenate([_pair(a, z), _pair(z, b)], axis=0)


def _mm2(a, b):
    return jnp.dot(a.astype(BF16), _block_diag(b.astype(BF16)), preferred_element_type=F32)


def _scatter_rows(t, stripes, n):
    parts, at, used = [], 0, 0
    for s in stripes:
        if s.start > at:
            parts.append(jnp.zeros((s.start - at, t.shape[1]), t.dtype))
        size = s.stop - s.start
        parts.append(t[used:used + size])
        used += size
        at = s.stop
    if at < n:
        parts.append(jnp.zeros((n - at, t.shape[1]), t.dtype))
    return jnp.concatenate(parts, axis=0)


def _unit_lower_inverses(lms, row, col):
    c = row.shape[0]
    eye = (row == col).astype(F32)
    base = SUBLANES
    same = lambda size: (row & -size) == (col & -size)
    lds = [jnp.where(same(base), lm, 0.0) for lm in lms]
    a1s = [eye - ld for ld in lds]
    ldbs = [ld.astype(BF16) for ld in lds]
    ld2bs = [_mm2(ldb, ldb).astype(BF16) for ldb in ldbs]
    ld4s = [_mm2(ld2b, ld2b) for ld2b in ld2bs]
    p1s = [a1 + _mm2(a1, ld2b) for a1, ld2b in zip(a1s, ld2bs)]
    xs = [p1 + _mm2(p1, ld4) for p1, ld4 in zip(p1s, ld4s)]
    lmbs = [lm.astype(BF16) for lm in lms]
    zero = jnp.zeros(row.shape, BF16)
    blk = base
    while blk < c:
        mask = same(2 * blk) & jnp.logical_not(same(blk))
        xbs = [x.astype(BF16) for x in xs]
        if blk % (2 * SUBLANES) == 0:
            stripes = [slice(r0, r0 + blk) for r0 in range(blk, c, 2 * blk)]
            rows_of = lambda t: jnp.concatenate([t[s] for s in stripes], axis=0)
            exs = [_mm2(rows_of(jnp.where(mask, lmb, zero)), xb) for lmb, xb in zip(lmbs, xbs)]
            upd = [_mm2(rows_of(xb), _scatter_rows(ex, stripes, c)) for xb, ex in zip(xbs, exs)]
            xs = [x - _scatter_rows(u, stripes, c) for x, u in zip(xs, upd)]
        else:
            exs = [_mm2(jnp.where(mask, lmb, zero), xb) for lmb, xb in zip(lmbs, xbs)]
            xs = [x - _mm2(xb, ex) for x, xb, ex in zip(xs, xbs, exs)]
        blk *= 2
    return xs


def _delta_kernel(q_ref, k_ref, v_ref, kt_ref, z_ref, sm_ref, alog_ref, dtb_ref, ng_ref, o_ref, s_ref, *, n_chunks):
    c = DELTA_CHUNK
    blk = n_chunks * c

    @pl.when(pl.program_id(1) == 0)
    def _():
        s_ref[...] = jnp.zeros(s_ref.shape, F32)

    sm = sm_ref[0]
    beta = jax.nn.sigmoid(sm[0:N_HEADS_B])
    log_beta = -jax.nn.softplus(-sm[0:N_HEADS_B])
    glog = -jnp.exp(alog_ref[...]) * jax.nn.softplus(sm[N_HEADS_B:2 * N_HEADS_B] + dtb_ref[...])
    gc, tail = _chunk_sums(glog)
    etail = jnp.exp(tail)
    egc = jnp.exp(gc)
    beta_egc = beta * egc
    rows = jnp.concatenate([gc + log_beta, gc, egc, jnp.exp(gc + tail),
                            jnp.zeros((LANES - 4 * N_HEADS_B, blk), F32)], axis=0)
    cols = rows.T

    n_pairs = N_HEADS_B // 2
    wide = 2 * LANES
    row_i = lax.broadcasted_iota(jnp.int32, (c, wide), 0)
    col_i = lax.broadcasted_iota(jnp.int32, (c, wide), 1) & (LANES - 1)
    causal = row_i >= col_i
    strict = row_i > col_i

    def col2(kind, p, rows_):
        base = kind * N_HEADS_B + 2 * p
        return _pair(jnp.broadcast_to(cols[rows_, base:base + 1], (rows_.stop - rows_.start, LANES)),
                     jnp.broadcast_to(cols[rows_, base + 1:base + 2], (rows_.stop - rows_.start, LANES)))

    def row2(arr, p, t):
        return _pair(arr[2 * p:2 * p + 1, t], arr[2 * p + 1:2 * p + 2, t])

    lanes_of = lambda p: slice(p * wide, (p + 1) * wide)

    def state_free_part(chunk_ids):
        items = [(p, ci) for ci in chunk_ids for p in range(n_pairs)]
        pt = [(p, slice(ci * c, (ci + 1) * c)) for p, ci in items]
        qs = [q_ref[0, t, lanes_of(p)] for p, t in pt]
        ks = [k_ref[0, t, lanes_of(p)] for p, t in pt]
        vs = [v_ref[0, t, lanes_of(p)] for p, t in pt]
        kts = [_pair(kt_ref[0, 2 * p * LANES:(2 * p + 1) * LANES, t],
                     kt_ref[0, (2 * p + 1) * LANES:(2 * p + 2) * LANES, t]) for p, t in pt]
        gcbs = [col2(0, p, t) for p, t in pt]
        gccs = [col2(1, p, t) for p, t in pt]
        egcs = [col2(2, p, t) for p, t in pt]
        gcrs = [row2(gc, p, t) for p, t in pt]
        etrs = [row2(etail, p, t) for p, t in pt]
        brs = [row2(beta, p, t) for p, t in pt]
        bers = [row2(beta_egc, p, t) for p, t in pt]

        prods = [jnp.dot(jnp.concatenate([k, q], axis=0), _block_diag(kt), preferred_element_type=F32)
                 for k, q, kt in zip(ks, qs, kts)]
        lms = [pr[:c] * jnp.exp(jnp.where(strict, gcb - gcr, NEG_INF)) for pr, gcb, gcr in zip(prods, gcbs, gcrs)]
        a_intras = [pr[c:] * jnp.exp(jnp.where(causal, gcc - gcr, NEG_INF)) for pr, gcc, gcr in zip(prods, gccs, gcrs)]
        t_invs = _unit_lower_inverses(lms, row_i, col_i)
        us = [jnp.dot((t_inv * br).astype(BF16), _block_diag(v), preferred_element_type=F32)
              for t_inv, br, v in zip(t_invs, brs, vs)]
        ws = [jnp.dot((t_inv * ber).astype(BF16), _block_diag(k), preferred_element_type=F32)
              for t_inv, ber, k in zip(t_invs, bers, ks)]
        wqs = [jnp.concatenate([w.astype(BF16), (q * egc).astype(BF16)], axis=0)
               for w, q, egc in zip(ws, qs, egcs)]
        akts = [jnp.concatenate([a.astype(BF16), (kt * etr).astype(BF16)], axis=0)
                for a, kt, etr in zip(a_intras, kts, etrs)]

        return us, wqs, akts

    us, wqs, akts = [], [], []
    for first in range(0, n_chunks, DELTA_GROUP):
        part = state_free_part(range(first, min(first + DELTA_GROUP, n_chunks)))
        us += part[0]
        wqs += part[1]
        akts += part[2]

    states = [s_ref[p] for p in range(n_pairs)]
    for ci in range(n_chunks):
        idx = [ci * n_pairs + p for p in range(n_pairs)]
        tok = slice(ci * c, (ci + 1) * c)
        r1 = [jnp.dot(wqs[i], _block_diag(states[p].astype(BF16)), preferred_element_type=F32)
              for p, i in enumerate(idx)]
        v_news = [us[i] - r[:c] for i, r in zip(idx, r1)]
        r2 = [jnp.dot(akts[i], _block_diag(vn.astype(BF16)), preferred_element_type=F32)
              for i, vn in zip(idx, v_news)]
        for p in range(n_pairs):
            base = 3 * N_HEADS_B + 2 * p
            egl = _pair(jnp.broadcast_to(cols[ci * c:ci * c + 1, base:base + 1], (1, LANES)),
                        jnp.broadcast_to(cols[ci * c:ci * c + 1, base + 1:base + 2], (1, LANES)))
            states[p] = states[p] * egl + r2[p][c:]
            o = r1[p][c:] + r2[p][:c]
            o = _pair(*[u * lax.rsqrt(jnp.mean(u * u, axis=-1, keepdims=True) + NORM_EPS) * ng_ref[...]
                        for u in _halves(o)])
            o_ref[0, tok, lanes_of(p)] = (o * z_ref[0, tok, lanes_of(p)]).astype(BF16)
    for p in range(n_pairs):
        s_ref[p] = states[p]


def _delta(hc, kt, small, a_log, dt_bias, norm_g, *, n_chunks):
    b, l, _ = hc.shape
    blk = n_chunks * DELTA_CHUNK
    col = lambda cidx: pl.BlockSpec((1, blk, QK_B), lambda i, j: (i, j, cidx))
    return pl.pallas_call(
        functools.partial(_delta_kernel, n_chunks=n_chunks),
        grid=(b, l // blk),
        in_specs=[col(COL_QB // QK_B), col(COL_KB // QK_B), col(COL_VB // QK_B),
                  pl.BlockSpec((1, QK_B, blk), lambda i, j: (i, 0, j)),
                  col(COL_Z // QK_B),
                  pl.BlockSpec((1, 2 * N_HEADS_B, blk), lambda i, j: (i, 0, j)),
                  _resident(a_log.shape), _resident(dt_bias.shape), _resident(norm_g.shape)],
        out_specs=pl.BlockSpec((1, blk, V_B), lambda i, j: (i, j, 0)),
        out_shape=jax.ShapeDtypeStruct((b, l, V_B), BF16),
        scratch_shapes=[pltpu.VMEM((N_HEADS_B // 2, KEY_DIM_B, 2 * VAL_DIM_B), F32)],
        compiler_params=_cparams(("parallel", "arbitrary")),
    )(hc, hc, hc, kt, hc, small, a_log, dt_bias, norm_g)


def _merge_kernel(x_ref, ya_ref, yb_ref, ga_ref, gb_ref, wa_ref, wb_ref, wo_ref, g_ref, b_ref, o_ref, *, alpha):
    half = x_ref.shape[0] // 2
    for part in range(2):
        rows = slice(part * half, (part + 1) * half)
        ya = jnp.dot(ya_ref[rows, :], wa_ref[...], preferred_element_type=F32)
        yb = jnp.dot(yb_ref[rows, :], wb_ref[...], preferred_element_type=F32)
        mix = (jax.nn.sigmoid(ga_ref[rows, :].astype(F32)) * ya
               + jax.nn.sigmoid(gb_ref[rows, :].astype(F32)) * yb)
        y = jnp.dot(mix.astype(BF16), wo_ref[...], preferred_element_type=F32)
        o_ref[rows, :] = _layernorm(alpha * x_ref[rows, :] + y, g_ref[...], b_ref[...])


def _merge(x2, ya, yb, hc2, wa, wb, wo, g, b, *, alpha, tm):
    t, d = x2.shape
    tile = lambda cidx=0: pl.BlockSpec((tm, d), lambda i: (i, cidx))
    return pl.pallas_call(
        functools.partial(_merge_kernel, alpha=alpha),
        grid=(t // tm,),
        in_specs=[tile(), tile(), tile(), tile(COL_GA // Q_A), tile(COL_GA // Q_A + 1),
                  _resident(wa.shape), _resident(wb.shape), _resident(wo.shape),
                  _resident(g.shape), _resident(b.shape)],
        out_specs=tile(),
        out_shape=jax.ShapeDtypeStruct((t, d), F32),
        compiler_params=_cparams(("parallel",)),
    )(x2, ya, yb, hc2, hc2, wa, wb, wo, g, b)


def _prep_w_in(w_in):
    d = w_in.shape[0]
    narrow = W_Z
    w_main = jnp.concatenate([w_in[:, :Q_A] * (HEAD_DIM_A ** -0.5 * LOG2E), w_in[:, Q_A:narrow],
                              w_in[:, narrow + 2 * N_HEADS_B:]], axis=1).astype(BF16)
    w_small = jnp.concatenate([w_in[:, narrow:narrow + 2 * N_HEADS_B],
                               jnp.zeros((d, LANES - 2 * N_HEADS_B), w_in.dtype)], axis=1).astype(BF16)
    return w_main, w_small


def _prep_ffn(w13, w2):
    f = w13.shape[1] // 2
    return w13[:, :f].astype(BF16), w13[:, f:].astype(BF16), w2.astype(BF16)


def kernel(x, rel_bias, ln_g, ln_b, ffn_w13, ffn_w2, w_in, conv_w, a_log, dt_bias, dn_norm_g, sinks,
           w_branch_a, w_branch_b, w_out):
    b, l, d = x.shape
    depth = w_in.shape[0]
    alpha = (2 * depth) ** 0.25
    t = b * l
    tm = 512
    pos_bias = _pos_bias(rel_bias)
    x2 = x.reshape(t, d)
    row = lambda v: v.reshape(1, -1)
    colv = lambda v: v.reshape(-1, 1)
    for i in range(depth):
        x2 = _ffn_ln(x2, *_prep_ffn(ffn_w13[i, 0], ffn_w2[i, 0]), row(ln_g[i, 0]), row(ln_b[i, 0]),
                     alpha=alpha, tm=2 * tm)
        w_main, w_small = _prep_w_in(w_in[i])
        hc, kt, small = _in_proj(x2.reshape(b, l, d), w_main, w_small, conv_w[i], tm=tm)
        ya = _attention(hc, pos_bias, sinks[i])
        yb = _delta(hc, kt, small, colv(a_log[i]), colv(dt_bias[i]), row(dn_norm_g[i]), n_chunks=8)
        x2 = _merge(x2, ya.reshape(t, Q_A), yb.reshape(t, V_B), hc.reshape(t, N_MAIN),
                    w_branch_a[i].astype(BF16), w_branch_b[i].astype(BF16), w_out[i].astype(BF16),
                    row(ln_g[i, 1]), row(ln_b[i, 1]), alpha=alpha, tm=2 * tm)
        x2 = _ffn_ln(x2, *_prep_ffn(ffn_w13[i, 1], ffn_w2[i, 1]), row(ln_g[i, 2]), row(ln_b[i, 2]),
                     alpha=alpha, tm=2 * tm)
    return x2.reshape(b, l, d)
```
